```python
import math, functools
import jax, jax.numpy as jnp
from jax import lax
import numpy as np

D_MODEL = 1024
BATCH = 4
SEQ = 4096
DEPTH = 1
DEC_BATCH = 32
DEC_SEQ = 8
PAST_LEN = 16384
PAGE_SIZE = 128

A_HEADS = 8
A_DK = D_MODEL // A_HEADS
A_DV = D_MODEL // A_HEADS
A_WIDTH = A_HEADS * A_DV
A_CHUNK = 64
B_HEADS = 8
B_DQK = D_MODEL // B_HEADS // 2
B_DV = 2 * B_DQK
B_WIDTH = B_HEADS * B_DV
Q_BLOCK = 128
MEM_LEN = 256
M_HEADS = 4
M_DH = D_MODEL // M_HEADS
M_WIDTH = M_HEADS * M_DH
D_FF = 4 * D_MODEL
EPS = 1e-6
IN_SIZES = (A_HEADS * A_DK, A_HEADS * A_DK, A_WIDTH, A_WIDTH,
            B_HEADS * 2 * B_DQK, B_HEADS * 2 * B_DQK, B_WIDTH,
            D_MODEL, D_MODEL)
IN_COLS = sum(IN_SIZES)

kernel_name = "hgrn2_diffattn_gated_hybrid_step"


def rmsnorm(x, w):
    xf = x.astype(jnp.float32)
    xf = xf * lax.rsqrt(jnp.mean(xf * xf, axis=-1, keepdims=True) + EPS)
    return xf.astype(x.dtype) * w


def hgrn2_recurrence(q, k, v, logf, s0):
    b, t, h, dk = q.shape
    dv = v.shape[-1]
    c = math.gcd(t, A_CHUNK)
    n = t // c

    def to_chunks(a):
        return jnp.moveaxis(a.astype(jnp.float32).reshape(b, n, c, h, a.shape[-1]), 1, 0)

    qc, kc, vc, gc = to_chunks(q), to_chunks(k), to_chunks(v), to_chunks(logf)
    causal = jnp.tril(jnp.ones((c, c), dtype=bool))[None, :, :, None, None]

    def step(s, inp):
        qi, ki, vi, gi = inp
        cum = jnp.cumsum(gi, axis=1)
        diff = cum[:, :, None] - cum[:, None, :]
        decay = jnp.where(causal, jnp.exp(jnp.minimum(diff, 0.0)), 0.0)
        scores = jnp.einsum('bthd,btshd,bshd->bhts', qi, decay, ki)
        o = (jnp.einsum('bhts,bshv->bthv', scores, vi)
             + jnp.einsum('bthd,bhdv->bthv', qi * jnp.exp(cum), s))
        last = cum[:, -1]
        kd = ki * jnp.exp(last[:, None] - cum)
        s_new = jnp.exp(last)[..., None] * s + jnp.einsum('bshd,bshv->bhdv', kd, vi)
        return s_new, o

    s_fin, oc = lax.scan(step, s0.astype(jnp.float32), (qc, kc, vc, gc))
    o = jnp.moveaxis(oc, 0, 1).reshape(b, t, h, dv)
    return o, s_fin


def diff_attention(q, k, v, lam, k_past, v_past):
    b, t, h = q.shape[:3]
    blk = math.gcd(t, Q_BLOCK)
    nb = t // blk
    qb = jnp.moveaxis(q.reshape(b, nb, blk, h, 2, B_DQK), 1, 0)
    qpos = jnp.arange(t, dtype=jnp.int32).reshape(nb, blk)
    kpos = jnp.arange(t, dtype=jnp.int32)
    scale = B_DQK ** -0.5
    n_past = 0 if k_past is None else k_past.shape[1]

    def one_block(args):
        qi, pi = args
        s = jnp.einsum('bqhcd,bkhcd->bhcqk', qi, k).astype(jnp.float32) * scale
        s = jnp.where(kpos[None, :] <= pi[:, None], s, -jnp.inf)
        if k_past is not None:
            sp = jnp.einsum('bqhcd,bkhcd->bhcqk', qi, k_past).astype(jnp.float32) * scale
            s = jnp.concatenate([sp, s], axis=-1)
        p = jax.nn.softmax(s, axis=-1)
        w = (p[:, :, 0] - lam * p[:, :, 1]).astype(v.dtype)
        o = jnp.einsum('bhqk,bkhv->bqhv', w[..., n_past:], v)
        if k_past is not None:
            o = o + jnp.einsum('bhqk,bkhv->bqhv', w[..., :n_past], v_past)
        return o

    o = lax.map(one_block, (qb, qpos))
    return jnp.moveaxis(o, 0, 1).reshape(b, t, h, v.shape[-1])


def mixer_block(xn, s0, k_past, v_past, lp, lb, lam, lam_init):
    b, t, _ = xn.shape
    proj = xn @ lp['w_in']
    splits = np.cumsum(IN_SIZES)[:-1].tolist()
    q_a, f_a, i_a, g_a, q_b, k_b, v_b, gate_a, gate_b = jnp.split(proj, splits, axis=-1)
    hq = jax.nn.silu(q_a).reshape(b, t, A_HEADS, A_DK)
    f = lb + (1.0 - lb) * jax.nn.sigmoid(f_a.astype(jnp.float32))
    logf = jnp.log(f).reshape(b, t, A_HEADS, A_DK)
    hk = (1.0 - f).reshape(b, t, A_HEADS, A_DK)
    hv = i_a.reshape(b, t, A_HEADS, A_DV)
    o_a, s_new = hgrn2_recurrence(hq, hk, hv, logf, s0)
    o_a = rmsnorm(o_a.astype(xn.dtype), lp['w_hgrn_gnorm']) * jax.nn.silu(g_a.reshape(b, t, A_HEADS, A_DV))
    o_a = o_a.reshape(b, t, A_WIDTH)
    dq = q_b.reshape(b, t, B_HEADS, 2, B_DQK)
    dk = k_b.reshape(b, t, B_HEADS, 2, B_DQK)
    dv = v_b.reshape(b, t, B_HEADS, B_DV)
    o_b = diff_attention(dq, dk, dv, lam, k_past, v_past)
    o_b = (rmsnorm(o_b, lp['w_subln']) * (1.0 - lam_init)).reshape(b, t, B_WIDTH)
    y = (jax.nn.sigmoid(gate_a) * (o_a @ lp['w_branch_a'])
         + jax.nn.sigmoid(gate_b) * (o_b @ lp['w_branch_b']))
    return y @ lp['w_out'], s_new.astype(xn.dtype), dk, dv


def memory_kv(mem, w_norm, w_mk, w_mv):
    b, m, _ = mem.shape
    mn = rmsnorm(mem, w_norm)
    return (mn @ w_mk).reshape(b, m, M_HEADS, M_DH), (mn @ w_mv).reshape(b, m, M_HEADS, M_DH)


def memory_attention(xn, mk, mv, w_mq, w_mo):
    b, t, _ = xn.shape
    q = (xn @ w_mq).reshape(b, t, M_HEADS, M_DH)
    s = jnp.einsum('bqhd,bkhd->bhqk', q, mk).astype(jnp.float32) * (M_DH ** -0.5)
    p = jax.nn.softmax(s, axis=-1).astype(mv.dtype)
    o = jnp.einsum('bhqk,bkhd->bqhd', p, mv).reshape(b, t, M_WIDTH)
    return o @ w_mo


def decoder_layer(x, s0, k_past, v_past, mk, mv, lp, lb, lam, lam_init):
    h = x
    mix, s_new, k_new, v_new = mixer_block(rmsnorm(h, lp['w_pre_mix']), s0, k_past, v_past, lp, lb, lam, lam_init)
    h = h + rmsnorm(mix, lp['w_post_mix'])
    mem_out = memory_attention(rmsnorm(h, lp['w_pre_mem']), mk, mv, lp['w_mq'], lp['w_mo'])
    h = h + rmsnorm(mem_out, lp['w_post_mem'])
    u = jnp.square(jax.nn.relu(rmsnorm(h, lp['w_pre_mlp']) @ lp['w_up']))
    h = h + rmsnorm(u @ lp['w_down'], lp['w_post_mlp'])
    return h, s_new, k_new, v_new


def setup_inputs(seed: int = 0) -> dict:
    key = jax.random.key(seed)
    keys = iter(jax.random.split(key, 48))
    f32 = jnp.float32

    def normal(shape, scale):
        return jax.random.normal(next(keys), shape, f32) * scale

    def gain(shape):
        return 1.0 + 0.05 * jax.random.normal(next(keys), shape, f32)

    n_pages = PAST_LEN // PAGE_SIZE
    n_pool = (DEC_BATCH * n_pages * 5) // 4
    perm = jax.random.permutation(next(keys), n_pool)[: DEC_BATCH * n_pages]
    page_table = perm.reshape(DEC_BATCH, n_pages).astype(jnp.int32)
    return {
        'x_prompt': normal((BATCH, SEQ, D_MODEL), 1.0),
        'x_sample': normal((DEC_BATCH, DEC_SEQ, D_MODEL), 1.0),
        'cache_attn_k': normal((DEPTH, n_pool, PAGE_SIZE, B_HEADS, 2, B_DQK), 1.0),
        'cache_attn_v': normal((DEPTH, n_pool, PAGE_SIZE, B_HEADS, B_DV), 1.0),
        'state_hgrn': normal((DEPTH, DEC_BATCH, A_HEADS, A_DK, A_DV), 0.5),
        'cache_mem_k': normal((DEPTH, DEC_BATCH, MEM_LEN, M_HEADS, M_DH), 1.0),
        'cache_mem_v': normal((DEPTH, DEC_BATCH, MEM_LEN, M_HEADS, M_DH), 1.0),
        'page_table': page_table,
        'mem_prompt': normal((BATCH, MEM_LEN, D_MODEL), 1.0),
        'w_pre_mix': gain((DEPTH, D_MODEL)),
        'w_in': normal((DEPTH, D_MODEL, IN_COLS), D_MODEL ** -0.5),
        'hgrn_lb_logits': normal((DEPTH + 1, A_HEADS * A_DK), 0.1),
        'w_hgrn_gnorm': gain((DEPTH, A_DV)),
        'lambda_q1': normal((DEPTH, B_DQK), 0.1),
        'lambda_k1': normal((DEPTH, B_DQK), 0.1),
        'lambda_q2': normal((DEPTH, B_DQK), 0.1),
        'lambda_k2': normal((DEPTH, B_DQK), 0.1),
        'w_subln': gain((DEPTH, B_DV)),
        'w_branch_a': normal((DEPTH, A_WIDTH, D_MODEL), A_WIDTH ** -0.5),
        'w_branch_b': normal((DEPTH, B_WIDTH, D_MODEL), B_WIDTH ** -0.5),
        'w_out': normal((DEPTH, D_MODEL, D_MODEL), D_MODEL ** -0.5),
        'w_post_mix': gain((DEPTH, D_MODEL)),
        'w_pre_mem': gain((DEPTH, D_MODEL)),
        'w_mem_norm': gain((DEPTH, D_MODEL)),
        'w_mq': normal((DEPTH, D_MODEL, M_WIDTH), D_MODEL ** -0.5),
        'w_mk': normal((DEPTH, D_MODEL, M_WIDTH), D_MODEL ** -0.5),
        'w_mv': normal((DEPTH, D_MODEL, M_WIDTH), D_MODEL ** -0.5),
        'w_mo': normal((DEPTH, M_WIDTH, D_MODEL), M_WIDTH ** -0.5),
        'w_post_mem': gain((DEPTH, D_MODEL)),
        'w_pre_mlp': gain((DEPTH, D_MODEL)),
        'w_up': normal((DEPTH, D_MODEL, D_FF), D_MODEL ** -0.5),
        'w_down': normal((DEPTH, D_FF, D_MODEL), D_FF ** -0.5),
        'w_post_mlp': gain((DEPTH, D_MODEL)),
    }


def reference(x_prompt, x_sample, cache_attn_k, cache_attn_v, state_hgrn, cache_mem_k, cache_mem_v,
              page_table, mem_prompt, w_pre_mix, w_in, hgrn_lb_logits, w_hgrn_gnorm,
              lambda_q1, lambda_k1, lambda_q2, lambda_k2, w_subln, w_branch_a, w_branch_b, w_out,
              w_post_mix, w_pre_mem, w_mem_norm, w_mq, w_mk, w_mv, w_mo, w_post_mem,
              w_pre_mlp, w_up, w_down, w_post_mlp):
    n_seq, n_pages = page_table.shape
    past_len = n_pages * cache_attn_k.shape[2]
    lb_all = jnp.cumsum(jax.nn.softmax(hgrn_lb_logits.astype(jnp.float32), axis=0), axis=0)
    hp, hs = x_prompt, x_sample
    kp_l, vp_l, sp_l, mkp_l, mvp_l, ks_l, vs_l, ss_l = [], [], [], [], [], [], [], []
    for l in range(DEPTH):
        lp = {
            'w_pre_mix': w_pre_mix[l], 'w_in': w_in[l], 'w_hgrn_gnorm': w_hgrn_gnorm[l],
            'w_subln': w_subln[l], 'w_branch_a': w_branch_a[l], 'w_branch_b': w_branch_b[l],
            'w_out': w_out[l], 'w_post_mix': w_post_mix[l], 'w_pre_mem': w_pre_mem[l],
            'w_mq': w_mq[l], 'w_mo': w_mo[l], 'w_post_mem': w_post_mem[l],
            'w_pre_mlp': w_pre_mlp[l], 'w_up': w_up[l], 'w_down': w_down[l], 'w_post_mlp': w_post_mlp[l],
        }
        lam_init = 0.8 - 0.6 * math.exp(-0.3 * l)
        lam = (jnp.exp(jnp.sum(lambda_q1[l].astype(jnp.float32) * lambda_k1[l].astype(jnp.float32)))
               - jnp.exp(jnp.sum(lambda_q2[l].astype(jnp.float32) * lambda_k2[l].astype(jnp.float32)))
               + lam_init)
        layer = functools.partial(decoder_layer, lp=lp, lb=lb_all[l], lam=lam, lam_init=lam_init)
        s0 = jnp.zeros((hp.shape[0], A_HEADS, A_DK, A_DV), hp.dtype)
        mk_p, mv_p = memory_kv(mem_prompt, w_mem_norm[l], w_mk[l], w_mv[l])
        hp, s_p, k_p, v_p = layer(hp, s0, None, None, mk_p, mv_p)
        k_past = cache_attn_k[l][page_table].reshape(n_seq, past_len, B_HEADS, 2, B_DQK)
        v_past = cache_attn_v[l][page_table].reshape(n_seq, past_len, B_HEADS, B_DV)
        hs, s_s, k_s, v_s = layer(hs, state_hgrn[l], k_past, v_past, cache_mem_k[l], cache_mem_v[l])
        kp_l.append(k_p); vp_l.append(v_p); sp_l.append(s_p); mkp_l.append(mk_p); mvp_l.append(mv_p)
        ks_l.append(k_s); vs_l.append(v_s); ss_l.append(s_s)
    return (hp, hs,
            jnp.stack(kp_l), jnp.stack(vp_l), jnp.stack(sp_l), jnp.stack(mkp_l), jnp.stack(mvp_l),
            jnp.stack(ks_l), jnp.stack(vs_l), jnp.stack(ss_l))
```

```python
import functools
import math

import jax
import jax.numpy as jnp
from jax import lax
from jax.experimental import pallas as pl
from jax.experimental.pallas import tpu as pltpu

F32 = jnp.float32
BF16 = jnp.bfloat16

D_MODEL = 1024
N_HEADS = 8
HEAD_DIM = 128
QK_DIM = 64
MEM_HEADS = 4
MEM_DIM = 256
D_FF = 4 * D_MODEL
EPS = 1e-6
HGRN_CHUNK = 64
HGRN_SUB = 8
VMEM_LIMIT_BYTES = 56 * 1024 * 1024

COL_HQ, COL_F, COL_I, COL_G, COL_QB, COL_KB, COL_VB, COL_GA, COL_GB = range(9)


def _sigmoid(x):
    return 1.0 / (1.0 + jnp.exp(-x))


def _rms(x, gain):
    ms = jnp.mean(x * x, axis=-1, keepdims=True)
    return x * lax.rsqrt(ms + EPS) * gain


def _compiler_params(semantics):
    return pltpu.CompilerParams(dimension_semantics=semantics, vmem_limit_bytes=VMEM_LIMIT_BYTES)


def _norm_matmul_kernel(x_ref, g_ref, w_ref, o_ref, *, act):
    xn = _rms(x_ref[...], g_ref[...]).astype(BF16)
    y = jnp.dot(xn, w_ref[...], preferred_element_type=F32)
    if act == "silu":
        y = y * _sigmoid(y)
    elif act == "sigmoid":
        y = _sigmoid(y)
    o_ref[...] = y


def _norm_matmul(x, gain, w, col0, ncols, act, tm):
    n, d = x.shape
    tn = D_MODEL
    assert n % tm == 0 and ncols % tn == 0 and col0 % tn == 0
    cb0 = col0 // tn
    return pl.pallas_call(
        functools.partial(_norm_matmul_kernel, act=act),
        grid=(ncols // tn, n // tm),
        in_specs=[
            pl.BlockSpec((tm, d), lambda j, i: (i, 0)),
            pl.BlockSpec((1, d), lambda j, i: (0, 0)),
            pl.BlockSpec((d, tn), lambda j, i: (0, cb0 + j)),
        ],
        out_specs=pl.BlockSpec((tm, tn), lambda j, i: (i, j)),
        out_shape=jax.ShapeDtypeStruct((n, ncols), F32),
        compiler_params=_compiler_params(("arbitrary", "arbitrary")),
        name="norm_matmul",
    )(x, gain, w)


def _hgrn_levels(chunk):
    levels = []
    b = chunk // 2
    while b >= HGRN_SUB:
        levels.append(b)
        b //= 2
    return levels


def _hgrn_kernel(hq_ref, fi_ref, sg_ref, lbl_ref, wg_ref, s0_ref, o_ref, sfin_ref, st_scr,
                 *, chunk, n_chunks, has_s0):
    t_blk = pl.program_id(1)
    hd = HEAD_DIM

    @pl.when(t_blk == 0)
    def _():
        for h in range(N_HEADS):
            if has_s0:
                st_scr[h] = s0_ref[0, h].T
            else:
                st_scr[h] = jnp.zeros((hd, hd), F32)

    lbl = lbl_ref[...]
    lmax = jnp.max(lbl, axis=0, keepdims=True)
    lexp = jnp.exp(lbl - lmax)
    lb_all = lexp[0:1, :] / jnp.sum(lexp, axis=0, keepdims=True)

    levels = _hgrn_levels(chunk)
    row = lax.broadcasted_iota(jnp.int32, (chunk, hd), 0)
    tril = (lax.broadcasted_iota(jnp.int32, (chunk, chunk), 0)
            >= lax.broadcasted_iota(jnp.int32, (chunk, chunk), 1)).astype(F32)
    rr = lax.broadcasted_iota(jnp.int32, (chunk, chunk), 0)
    cc = lax.broadcasted_iota(jnp.int32, (chunk, chunk), 1)
    is_right = {b: (row & b) != 0 for b in levels}
    same_pair = {b: (rr // (2 * b)) == (cc // (2 * b)) for b in levels}
    sub_row = lax.broadcasted_iota(jnp.int32, (HGRN_SUB, hd), 0)
    wg = wg_ref[...]

    def chunk_body(c, carry):
        r0 = pl.multiple_of(c * chunk, chunk)
        rows = pl.ds(r0, chunk)
        for h in range(N_HEADS):
            cols = slice(h * hd, (h + 1) * hd)
            lb = lb_all[:, cols]
            q = hq_ref[0, rows, cols]
            fa = fi_ref[0, rows, cols]
            v = fi_ref[0, rows, D_MODEL + h * hd:D_MODEL + (h + 1) * hd]
            f = lb + (1.0 - lb) * _sigmoid(fa)
            g = jnp.log(f)
            k = 1.0 - f
            cum = jnp.dot(tril, g, preferred_element_type=F32, precision=lax.Precision.HIGHEST)
            v_b = v.astype(BF16)

            o = jnp.zeros((chunk, hd), F32)
            if levels:
                a = jnp.zeros((chunk, chunk), F32)
                for b in levels:
                    pieces = []
                    for p in range(chunk // (2 * b)):
                        ref_row = cum[p * 2 * b + b - 1:p * 2 * b + b, :]
                        pieces.append(jnp.broadcast_to(ref_row, (2 * b, hd)))
                    ref = pieces[0] if len(pieces) == 1 else jnp.concatenate(pieces, axis=0)
                    e = jnp.exp(-jnp.abs(cum - ref))
                    ql = (q * jnp.where(is_right[b], e, 0.0)).astype(BF16)
                    kl = (k * jnp.where(is_right[b], 0.0, e)).astype(BF16)
                    al = lax.dot_general(ql, kl, (((1,), (1,)), ((), ())), preferred_element_type=F32)
                    a = a + jnp.where(same_pair[b], al, 0.0)
                o = jnp.dot(a.astype(BF16), v_b, preferred_element_type=F32)

            diag = []
            for j in range(chunk // HGRN_SUB):
                sl = slice(j * HGRN_SUB, (j + 1) * HGRN_SUB)
                qj, kj, cj, vj = q[sl], k[sl], cum[sl], v[sl]
                od = jnp.zeros((HGRN_SUB, hd), F32)
                for s in range(HGRN_SUB):
                    dec = jnp.exp(jnp.minimum(cj - cj[s:s + 1, :], 0.0))
                    dec = jnp.where(sub_row >= s, dec, 0.0)
                    a_s = jnp.sum(qj * dec * kj[s:s + 1, :], axis=-1, keepdims=True)
                    od = od + a_s * vj[s:s + 1, :]
                diag.append(od)
            o = o + (diag[0] if len(diag) == 1 else jnp.concatenate(diag, axis=0))

            st = st_scr[h]
            q0 = (q * jnp.exp(cum)).astype(BF16)
            o = o + lax.dot_general(q0, st.astype(BF16), (((1,), (1,)), ((), ())),
                                    preferred_element_type=F32)
            last = cum[chunk - 1:chunk, :]
            kd = (k * jnp.exp(last - cum)).astype(BF16)
            upd = jnp.dot(v.T.astype(BF16), kd, preferred_element_type=F32)
            st_scr[h] = st * jnp.exp(last) + upd

            o_ref[0, rows, cols] = _rms(o, wg) * sg_ref[0, rows, cols]
        return carry

    lax.fori_loop(0, n_chunks, chunk_body, 0)

    @pl.when(t_blk == pl.num_programs(1) - 1)
    def _():
        for h in range(N_HEADS):
            sfin_ref[0, h] = st_scr[h].T


def _hgrn(hq, fi, sg, lb_logits, w_gnorm, s0, batch, seq):
    chunk = math.gcd(seq, HGRN_CHUNK)
    assert chunk % HGRN_SUB == 0
    tb = math.gcd(seq, 512)
    n_tb = seq // tb
    hq3 = hq.reshape(batch, seq, D_MODEL)
    fi3 = fi.reshape(batch, seq, 2 * D_MODEL)
    sg3 = sg.reshape(batch, seq, D_MODEL)
    has_s0 = s0 is not None
    if not has_s0:
        s0 = jnp.zeros((1, N_HEADS, HEAD_DIM, HEAD_DIM), F32)
    s0_map = (lambda b, t: (b, 0, 0, 0)) if has_s0 else (lambda b, t: (0, 0, 0, 0))
    n_lb = lb_logits.shape[0]
    o, s_fin = pl.pallas_call(
        functools.partial(_hgrn_kernel, chunk=chunk, n_chunks=tb // chunk, has_s0=has_s0),
        grid=(batch, n_tb),
        in_specs=[
            pl.BlockSpec((1, tb, D_MODEL), lambda b, t: (b, t, 0)),
            pl.BlockSpec((1, tb, 2 * D_MODEL), lambda b, t: (b, t, 0)),
            pl.BlockSpec((1, tb, D_MODEL), lambda b, t: (b, t, 0)),
            pl.BlockSpec((n_lb, D_MODEL), lambda b, t: (0, 0)),
            pl.BlockSpec((1, HEAD_DIM), lambda b, t: (0, 0)),
            pl.BlockSpec((1, N_HEADS, HEAD_DIM, HEAD_DIM), s0_map),
        ],
        out_specs=[
            pl.BlockSpec((1, tb, D_MODEL), lambda b, t: (b, t, 0)),
            pl.BlockSpec((1, N_HEADS, HEAD_DIM, HEAD_DIM), lambda b, t: (b, 0, 0, 0)),
        ],
        out_shape=[
            jax.ShapeDtypeStruct((batch, seq, D_MODEL), F32),
            jax.ShapeDtypeStruct((batch, N_HEADS, HEAD_DIM, HEAD_DIM), F32),
        ],
        scratch_shapes=[pltpu.VMEM((N_HEADS, HEAD_DIM, HEAD_DIM), F32)],
        compiler_params=_compiler_params(("arbitrary", "arbitrary")),
        name="hgrn2",
    )(hq3, fi3, sg3, lb_logits, w_gnorm, s0)
    return o.reshape(batch * seq, D_MODEL), s_fin


def _lambda_value(lam_ref, lam_init):
    l = lam_ref[...]
    a = jnp.sum(l[0:1, :] * l[1:2, :], axis=-1, keepdims=True)
    b = jnp.sum(l[2:3, :] * l[3:4, :], axis=-1, keepdims=True)
    return jnp.exp(a) - jnp.exp(b) + lam_init


def _prompt_attn_kernel(q_ref, k_ref, v_ref, lam_ref, wsub_ref, o_ref, kb_scr, vb_scr, acc_scr,
                        *, tq, n_q, lam_init):
    kb_scr[...] = k_ref[0].astype(BF16)
    vb_scr[...] = v_ref[0].astype(BF16)
    lam = _lambda_value(lam_ref, lam_init)
    wsub = wsub_ref[...]
    lane = lax.broadcasted_iota(jnp.int32, (tq, HEAD_DIM), 1)
    scale = QK_DIM ** -0.5
    tri = (lax.broadcasted_iota(jnp.int32, (2 * tq, tq), 1)
           <= (lax.broadcasted_iota(jnp.int32, (2 * tq, tq), 0) % tq))

    def q_body(qi, carry):
        q0 = pl.multiple_of(qi * tq, tq)
        q = q_ref[0, pl.ds(q0, tq), :] * scale
        qq = jnp.concatenate([jnp.where(lane < QK_DIM, q, 0.0), jnp.where(lane >= QK_DIM, q, 0.0)],
                             axis=0).astype(BF16)
        acc_scr[...] = jnp.zeros((2 * tq, HEAD_DIM), F32)

        def kv_step(j, ml, masked):
            m, l = ml
            k0 = pl.multiple_of(j * tq, tq)
            kb = kb_scr[pl.ds(k0, tq), :]
            vb = vb_scr[pl.ds(k0, tq), :]
            s = lax.dot_general(qq, kb, (((1,), (1,)), ((), ())), preferred_element_type=F32)
            if masked:
                s = jnp.where(tri, s, -jnp.inf)
            m_new = jnp.maximum(m, jnp.max(s, axis=-1, keepdims=True))
            alpha = jnp.exp(m - m_new)
            p = jnp.exp(s - m_new)
            l_new = alpha * l + jnp.sum(p, axis=-1, keepdims=True)
            acc_scr[...] = alpha * acc_scr[...] + jnp.dot(p.astype(BF16), vb, preferred_element_type=F32)
            return m_new, l_new

        init = (jnp.full((2 * tq, 1), -jnp.inf, F32), jnp.zeros((2 * tq, 1), F32))
        ml = lax.fori_loop(0, qi, lambda j, ml: kv_step(j, ml, False), init)
        m, l = kv_step(qi, ml, True)
        on = acc_scr[...] / l
        o = on[0:tq, :] - lam * on[tq:2 * tq, :]
        o_ref[0, pl.ds(q0, tq), :] = _rms(o, wsub) * (1.0 - lam_init)
        return carry

    lax.fori_loop(0, n_q, q_body, 0)


def _prompt_attn(qb, kb, vb, lam_params, w_subln, batch, seq, lam_init):
    tq = math.gcd(seq, 256)
    q3 = qb.reshape(batch, seq, D_MODEL)
    k3 = kb.reshape(batch, seq, D_MODEL)
    v3 = vb.reshape(batch, seq, D_MODEL)
    blk = pl.BlockSpec((1, seq, HEAD_DIM), lambda b, h: (b, 0, h))
    o = pl.pallas_call(
        functools.partial(_prompt_attn_kernel, tq=tq, n_q=seq // tq, lam_init=lam_init),
        grid=(batch, N_HEADS),
        in_specs=[blk, blk, blk,
                  pl.BlockSpec((4, QK_DIM), lambda b, h: (0, 0)),
                  pl.BlockSpec((1, HEAD_DIM), lambda b, h: (0, 0))],
        out_specs=blk,
        out_shape=jax.ShapeDtypeStruct((batch, seq, D_MODEL), F32),
        scratch_shapes=[pltpu.VMEM((seq, HEAD_DIM), BF16), pltpu.VMEM((seq, HEAD_DIM), BF16),
                        pltpu.VMEM((2 * tq, HEAD_DIM), F32)],
        compiler_params=_compiler_params(("arbitrary", "arbitrary")),
        name="diff_attn_prompt",
    )(q3, k3, v3, lam_params, w_subln)
    return o.reshape(batch * seq, D_MODEL)


def _sample_attn_kernel(pt_ref, q_ref, kn_ref, vn_ref, lam_ref, wsub_ref, *rest,
                        pages_per_step, n_new, lam_init):
    del pt_ref
    k_pages = rest[:pages_per_step]
    v_pages = rest[pages_per_step:2 * pages_per_step]
    o_ref, qrows_scr, m_scr, l_scr, acc_scr = rest[2 * pages_per_step:]
    step = pl.program_id(1)
    ncol = N_HEADS * 2 * n_new
    scale = QK_DIM ** -0.5

    @pl.when(step == 0)
    def _():
        q = q_ref[0] * scale
        qt = jnp.concatenate([q] * (ncol // n_new), axis=0)
        r = lax.broadcasted_iota(jnp.int32, (ncol, D_MODEL), 0)
        c = lax.broadcasted_iota(jnp.int32, (ncol, D_MODEL), 1)
        qrows_scr[...] = jnp.where((c // QK_DIM) == (r // n_new), qt, 0.0).astype(BF16)
        m_scr[...] = jnp.full((1, ncol), -jnp.inf, F32)
        l_scr[...] = jnp.zeros((1, ncol), F32)
        acc_scr[...] = jnp.zeros((ncol, D_MODEL), F32)

    def to_column(rowvec):
        return jnp.broadcast_to(rowvec, (ncol, ncol)).T

    def accumulate(scores, values):
        m_old = m_scr[...]
        m_new = m_old
        for s in scores:
            m_new = jnp.maximum(m_new, jnp.max(s, axis=0, keepdims=True))
        alpha = jnp.exp(m_old - m_new)
        l_new = alpha * l_scr[...]
        pv = None
        for s, v in zip(scores, values):
            p = jnp.exp(s - m_new)
            l_new = l_new + jnp.sum(p, axis=0, keepdims=True)
            d = lax.dot_general(p.astype(v.dtype), v, (((0,), (0,)), ((), ())), preferred_element_type=F32)
            pv = d if pv is None else pv + d
        alpha_col = to_column(alpha)
        for h in range(N_HEADS):
            cols = slice(h * HEAD_DIM, (h + 1) * HEAD_DIM)
            acc_scr[:, cols] = acc_scr[:, cols] * alpha_col + pv[:, cols]
        m_scr[...] = m_new
        l_scr[...] = l_new

    qrows = qrows_scr[...]
    scores = [lax.dot_general(kp[0].astype(BF16), qrows, (((1,), (1,)), ((), ())), preferred_element_type=F32)
              for kp in k_pages]
    accumulate(scores, [vp[0].astype(BF16) for vp in v_pages])

    @pl.when(step == pl.num_programs(1) - 1)
    def _():
        s_new = lax.dot_general(kn_ref[0].astype(BF16).astype(F32), qrows_scr[...].astype(F32),
                                (((1,), (1,)), ((), ())), preferred_element_type=F32)
        key = lax.broadcasted_iota(jnp.int32, (n_new, ncol), 0)
        qry = lax.broadcasted_iota(jnp.int32, (n_new, ncol), 1) % n_new
        s_new = jnp.where(key <= qry, s_new, -jnp.inf)
        accumulate([s_new], [vn_ref[0]])

        lam = _lambda_value(lam_ref, lam_init)
        wsub = wsub_ref[...]
        l_col = to_column(l_scr[...])
        for h in range(N_HEADS):
            cols = slice(h * HEAD_DIM, (h + 1) * HEAD_DIM)
            r1 = slice(h * 2 * n_new, h * 2 * n_new + n_new)
            r2 = slice(h * 2 * n_new + n_new, (h + 1) * 2 * n_new)
            o1 = acc_scr[r1, cols] / l_col[r1, :]
            o2 = acc_scr[r2, cols] / l_col[r2, :]
            o_ref[0, :, cols] = _rms(o1 - lam * o2, wsub) * (1.0 - lam_init)


def _sample_attn(qb, kb, vb, cache_k, cache_v, page_table, lam_params, w_subln, n_seq, n_new, lam_init):
    n_pages = page_table.shape[1]
    n_pool, page_size = cache_k.shape[0], cache_k.shape[1]
    pages_per_step = math.gcd(n_pages, 4)
    ncol = N_HEADS * 2 * n_new
    assert ncol == HEAD_DIM and n_new % 8 == 0
    ck = cache_k.reshape(n_pool, page_size, D_MODEL)
    cv = cache_v.reshape(n_pool, page_size, D_MODEL)
    q3 = qb.reshape(n_seq, n_new, D_MODEL)
    k3 = kb.reshape(n_seq, n_new, D_MODEL)
    v3 = vb.reshape(n_seq, n_new, D_MODEL)
    seq_blk = pl.BlockSpec((1, n_new, D_MODEL), lambda n, s, pt: (n, 0, 0))

    def page_spec(i):
        return pl.BlockSpec((1, page_size, D_MODEL),
                            lambda n, s, pt: (pt[n, s * pages_per_step + i], 0, 0))

    grid_spec = pltpu.PrefetchScalarGridSpec(
        num_scalar_prefetch=1,
        grid=(n_seq, n_pages // pages_per_step),
        in_specs=[seq_blk, seq_blk, seq_blk,
                  pl.BlockSpec((4, QK_DIM), lambda n, s, pt: (0, 0)),
                  pl.BlockSpec((1, HEAD_DIM), lambda n, s, pt: (0, 0))]
                 + [page_spec(i) for i in range(pages_per_step)]
                 + [page_spec(i) for i in range(pages_per_step)],
        out_specs=seq_blk,
        scratch_shapes=[pltpu.VMEM((ncol, D_MODEL), BF16), pltpu.VMEM((1, ncol), F32),
                        pltpu.VMEM((1, ncol), F32), pltpu.VMEM((ncol, D_MODEL), F32)],
    )
    o = pl.pallas_call(
        functools.partial(_sample_attn_kernel, pages_per_step=pages_per_step, n_new=n_new,
                          lam_init=lam_init),
        grid_spec=grid_spec,
        out_shape=jax.ShapeDtypeStruct((n_seq, n_new, D_MODEL), F32),
        compiler_params=_compiler_params(("arbitrary", "arbitrary")),
        name="diff_attn_sample",
    )(page_table, q3, k3, v3, lam_params, w_subln, *([ck] * pages_per_step), *([cv] * pages_per_step))
    return o.reshape(n_seq * n_new, D_MODEL)


def _merge_kernel(oa_ref, ob_ref, gate_ref, x_ref, wa_ref, wb_ref, wo_ref, g_ref, h_ref):
    ya = jnp.dot(oa_ref[...].astype(BF16), wa_ref[...], preferred_element_type=F32)
    yb = jnp.dot(ob_ref[...].astype(BF16), wb_ref[...], preferred_element_type=F32)
    y = gate_ref[:, 0:D_MODEL] * ya + gate_ref[:, D_MODEL:2 * D_MODEL] * yb
    mix = jnp.dot(y.astype(BF16), wo_ref[...], preferred_element_type=F32)
    h_ref[...] = x_ref[...] + _rms(mix, g_ref[...])


def _merge(oa, ob, gates, x, wa, wb, wo, g_post, tm):
    n = x.shape[0]
    row = pl.BlockSpec((tm, D_MODEL), lambda i: (i, 0))
    wspec = pl.BlockSpec((D_MODEL, D_MODEL), lambda i: (0, 0))
    return pl.pallas_call(
        _merge_kernel,
        grid=(n // tm,),
        in_specs=[row, row, pl.BlockSpec((tm, 2 * D_MODEL), lambda i: (i, 0)), row,
                  wspec, wspec, wspec, pl.BlockSpec((1, D_MODEL), lambda i: (0, 0))],
        out_specs=row,
        out_shape=jax.ShapeDtypeStruct((n, D_MODEL), F32),
        compiler_params=_compiler_params(("arbitrary",)),
        name="merge_out",
    )(oa, ob, gates, x, wa, wb, wo, g_post)


def _mem_attn_kernel(h_ref, mk_ref, mv_ref, gpre_ref, wq_ref, wo_ref, gpost_ref, o_ref):
    h = h_ref[...]
    q = jnp.dot(_rms(h, gpre_ref[...]).astype(BF16), wq_ref[...], preferred_element_type=F32)
    q = (q * (MEM_DIM ** -0.5)).astype(BF16)
    outs = []
    for hh in range(MEM_HEADS):
        cols = slice(hh * MEM_DIM, (hh + 1) * MEM_DIM)
        s = lax.dot_general(q[:, cols], mk_ref[0, :, cols].astype(BF16), (((1,), (1,)), ((), ())),
                            preferred_element_type=F32)
        p = jnp.exp(s - jnp.max(s, axis=-1, keepdims=True))
        p = p / jnp.sum(p, axis=-1, keepdims=True)
        outs.append(jnp.dot(p.astype(BF16), mv_ref[0, :, cols].astype(BF16), preferred_element_type=F32))
    o = jnp.concatenate(outs, axis=-1).astype(BF16)
    mo = jnp.dot(o, wo_ref[...], preferred_element_type=F32)
    o_ref[...] = h + _rms(mo, gpost_ref[...])


def _mem_attn(h, mk, mv, g_pre, wq, wo, g_post, rows_per_batch, tm):
    n = h.shape[0]
    mem_len = mk.shape[1]
    tiles_per_batch = rows_per_batch // tm
    row = pl.BlockSpec((tm, D_MODEL), lambda i: (i, 0))
    mem = pl.BlockSpec((1, mem_len, D_MODEL), lambda i: (i // tiles_per_batch, 0, 0))
    wspec = pl.BlockSpec((D_MODEL, D_MODEL), lambda i: (0, 0))
    gspec = pl.BlockSpec((1, D_MODEL), lambda i: (0, 0))
    return pl.pallas_call(
        _mem_attn_kernel,
        grid=(n // tm,),
        in_specs=[row, mem, mem, gspec, wspec, wspec, gspec],
        out_specs=row,
        out_shape=jax.ShapeDtypeStruct((n, D_MODEL), F32),
        compiler_params=_compiler_params(("arbitrary",)),
        name="mem_attn",
    )(h, mk, mv, g_pre, wq, wo, g_post)


def _mlp_kernel(h_ref, gpre_ref, wup_ref, wdown_ref, gpost_ref, o_ref, xn_scr, acc_scr):
    j = pl.program_id(1)

    @pl.when(j == 0)
    def _():
        xn_scr[...] = _rms(h_ref[...], gpre_ref[...]).astype(BF16)
        acc_scr[...] = jnp.zeros_like(acc_scr)

    u = jnp.dot(xn_scr[...], wup_ref[...], preferred_element_type=F32)
    u = jnp.square(jnp.maximum(u, 0.0)).astype(BF16)
    acc_scr[...] += jnp.dot(u, wdown_ref[...], preferred_element_type=F32)

    @pl.when(j == pl.num_programs(1) - 1)
    def _():
        o_ref[...] = h_ref[...] + _rms(acc_scr[...], gpost_ref[...])


def _mlp(h, g_pre, wup, wdown, g_post, tm):
    n = h.shape[0]
    tf = D_MODEL
    row = pl.BlockSpec((tm, D_MODEL), lambda i, j: (i, 0))
    gspec = pl.BlockSpec((1, D_MODEL), lambda i, j: (0, 0))
    return pl.pallas_call(
        _mlp_kernel,
        grid=(n // tm, D_FF // tf),
        in_specs=[row, gspec,
                  pl.BlockSpec((D_MODEL, tf), lambda i, j: (0, j)),
                  pl.BlockSpec((tf, D_MODEL), lambda i, j: (j, 0)),
                  gspec],
        out_specs=row,
        out_shape=jax.ShapeDtypeStruct((n, D_MODEL), F32),
        scratch_shapes=[pltpu.VMEM((tm, D_MODEL), BF16), pltpu.VMEM((tm, D_MODEL), F32)],
        compiler_params=_compiler_params(("arbitrary", "arbitrary")),
        name="mlp",
    )(h, g_pre, wup, wdown, g_post)


def _layer(x, batch, seq, s0, past, mem_kv, p, lam_init):
    n = x.shape[0]
    tm = math.gcd(n, 512)
    w_in = p["w_in"]
    proj = functools.partial(_norm_matmul, x, p["w_pre_mix"], w_in, tm=tm)
    hq = proj(COL_HQ * D_MODEL, D_MODEL, "silu")
    fi = proj(COL_F * D_MODEL, 2 * D_MODEL, "none")
    sg = proj(COL_G * D_MODEL, D_MODEL, "silu")
    qb = proj(COL_QB * D_MODEL, D_MODEL, "none")
    kb = proj(COL_KB * D_MODEL, D_MODEL, "none")
    vb = proj(COL_VB * D_MODEL, D_MODEL, "none")
    gates = proj(COL_GA * D_MODEL, 2 * D_MODEL, "sigmoid")

    o_a, s_new = _hgrn(hq, fi, sg, p["lb_logits"], p["w_hgrn_gnorm"], s0, batch, seq)
    if past is None:
        o_b = _prompt_attn(qb, kb, vb, p["lam_params"], p["w_subln"], batch, seq, lam_init)
    else:
        o_b = _sample_attn(qb, kb, vb, past[0], past[1], past[2], p["lam_params"], p["w_subln"],
                           batch, seq, lam_init)

    h = _merge(o_a, o_b, gates, x, p["w_branch_a"], p["w_branch_b"], p["w_out"], p["w_post_mix"], tm)
    mk, mv = mem_kv
    tm_mem = math.gcd(seq, 512)
    h = _mem_attn(h, mk, mv, p["w_pre_mem"], p["w_mq"], p["w_mo"], p["w_post_mem"], seq, tm_mem)
    h = _mlp(h, p["w_pre_mlp"], p["w_up"], p["w_down"], p["w_post_mlp"], tm)
    return h, s_new, kb, vb


def kernel(x_prompt, x_sample, cache_attn_k, cache_attn_v, state_hgrn, cache_mem_k, cache_mem_v,
           page_table, mem_prompt, w_pre_mix, w_in, hgrn_lb_logits, w_hgrn_gnorm,
           lambda_q1, lambda_k1, lambda_q2, lambda_k2, w_subln, w_branch_a, w_branch_b, w_out,
           w_post_mix, w_pre_mem, w_mem_norm, w_mq, w_mk, w_mv, w_mo, w_post_mem,
           w_pre_mlp, w_up, w_down, w_post_mlp):
    depth = w_in.shape[0]
    assert depth == 1, "single-layer step"
    l = 0
    batch, seq, _ = x_prompt.shape
    n_seq, n_new, _ = x_sample.shape
    mem_len = mem_prompt.shape[1]
    lam_init = 0.8 - 0.6 * math.exp(-0.3 * l)

    def gain(w):
        return w[l].reshape(1, -1)

    def weight(w):
        return w[l].astype(BF16)

    p = {
        "w_pre_mix": gain(w_pre_mix), "w_in": weight(w_in), "lb_logits": hgrn_lb_logits,
        "w_hgrn_gnorm": gain(w_hgrn_gnorm), "w_subln": gain(w_subln),
        "lam_params": jnp.stack([lambda_q1[l], lambda_k1[l], lambda_q2[l], lambda_k2[l]]),
        "w_branch_a": weight(w_branch_a), "w_branch_b": weight(w_branch_b), "w_out": weight(w_out),
        "w_post_mix": gain(w_post_mix), "w_pre_mem": gain(w_pre_mem), "w_mq": weight(w_mq),
        "w_mo": weight(w_mo), "w_post_mem": gain(w_post_mem), "w_pre_mlp": gain(w_pre_mlp),
        "w_up": weight(w_up), "w_down": weight(w_down), "w_post_mlp": gain(w_post_mlp),
    }

    memx = mem_prompt.reshape(batch * mem_len, D_MODEL)
    tm_mem = math.gcd(batch * mem_len, 512)
    mk_p = _norm_matmul(memx, gain(w_mem_norm), weight(w_mk), 0, D_MODEL, "none", tm_mem)
    mv_p = _norm_matmul(memx, gain(w_mem_norm), weight(w_mv), 0, D_MODEL, "none", tm_mem)
    mk_p3 = mk_p.reshape(batch, mem_len, D_MODEL)
    mv_p3 = mv_p.reshape(batch, mem_len, D_MODEL)
    hp, s_p, k_p, v_p = _layer(x_prompt.reshape(batch * seq, D_MODEL), batch, seq, None, None,
                               (mk_p3, mv_p3), p, lam_init)

    past = (cache_attn_k[l], cache_attn_v[l], page_table)
    mem_s = (cache_mem_k[l].reshape(n_seq, -1, D_MODEL), cache_mem_v[l].reshape(n_seq, -1, D_MODEL))
    hs, s_s, k_s, v_s = _layer(x_sample.reshape(n_seq * n_new, D_MODEL), n_seq, n_new, state_hgrn[l],
                               past, mem_s, p, lam_init)

    return (hp.reshape(batch, seq, D_MODEL), hs.reshape(n_seq, n_new, D_MODEL),
            k_p.reshape(1, batch, seq, N_HEADS, 2, QK_DIM), v_p.reshape(1, batch, seq, N_HEADS, HEAD_DIM),
            s_p[None], mk_p3.reshape(1, batch, mem_len, MEM_HEADS, MEM_DIM),
            mv_p3.reshape(1, batch, mem_len, MEM_HEADS, MEM_DIM),
            k_s.reshape(1, n_seq, n_new, N_HEADS, 2, QK_DIM), v_s.reshape(1, n_seq, n_new, N_HEADS, HEAD_DIM),
            s_s[None])
```

```python
import functools
import math

import jax
import jax.numpy as jnp
from jax import lax
from jax.experimental import pallas as pl
from jax.experimental.pallas import tpu as pltpu

F32 = jnp.float32
BF16 = jnp.bfloat16

D_MODEL = 1024
N_HEADS = 8
HEAD_DIM = 128
QK_DIM = 64
MEM_HEADS = 4
MEM_DIM = 256
D_FF = 4 * D_MODEL
EPS = 1e-6
LOG2E = math.log2(math.e)
HGRN_CHUNK = 64
HGRN_SUB = 8
VMEM_LIMIT_BYTES = 56 * 1024 * 1024

COL_HQ, COL_F, COL_I, COL_G, COL_QB, COL_KB, COL_VB, COL_GA, COL_GB = range(9)
N_GROUPS = 9


def _sigmoid(x):
    return 1.0 / (1.0 + jnp.exp(-x))


def _rms(x, gain):
    ms = jnp.mean(x * x, axis=-1, keepdims=True)
    return x * lax.rsqrt(ms + EPS) * gain


def _compiler_params(semantics):
    return pltpu.CompilerParams(dimension_semantics=semantics, vmem_limit_bytes=VMEM_LIMIT_BYTES)


def _norm_matmul_kernel(x_ref, g_ref, w_ref, o_ref):
    xn = _rms(x_ref[...], g_ref[...]).astype(BF16)
    o_ref[...] = jnp.dot(xn, w_ref[...], preferred_element_type=F32)


def _norm_matmul(x, gain, w, tm):
    n, d = x.shape
    return pl.pallas_call(
        _norm_matmul_kernel,
        grid=(n // tm,),
        in_specs=[
            pl.BlockSpec((tm, d), lambda i: (i, 0)),
            pl.BlockSpec((1, d), lambda i: (0, 0)),
            pl.BlockSpec((d, D_MODEL), lambda i: (0, 0)),
        ],
        out_specs=pl.BlockSpec((tm, D_MODEL), lambda i: (i, 0)),
        out_shape=jax.ShapeDtypeStruct((n, D_MODEL), F32),
        compiler_params=_compiler_params(("arbitrary",)),
        name="norm_matmul",
    )(x, gain, w)


def _in_proj_kernel(x_ref, g_ref, w_ref, o_ref, *rest, transpose_k):
    if transpose_k:
        kt_ref, xn_scr = rest
    else:
        (xn_scr,) = rest
    j = pl.program_id(1)

    @pl.when(j == 0)
    def _():
        xn_scr[...] = _rms(x_ref[...], g_ref[...]).astype(BF16)

    y = jnp.dot(xn_scr[...], w_ref[...], preferred_element_type=F32)
    is_silu = (j == COL_HQ) | (j == COL_G)
    is_sigmoid = (j == COL_GA) | (j == COL_GB)
    sig = 0.5 * jnp.tanh(0.5 * y) + 0.5
    o_ref[0] = jnp.where(is_silu, y * sig, jnp.where(is_sigmoid, sig, y))

    if transpose_k:
        @pl.when(j == COL_KB)
        def _():
            kt_ref[0] = y.T


def _in_proj(x, gain, w_in, batch, seq, transpose_k):
    n = x.shape[0]
    tm = math.gcd(seq, 512) if transpose_k else math.gcd(n, 512)
    tiles_per_batch = seq // tm if transpose_k else 1
    out_specs = [pl.BlockSpec((1, tm, D_MODEL), lambda i, j: (j, i, 0))]
    out_shape = [jax.ShapeDtypeStruct((N_GROUPS, n, D_MODEL), F32)]
    if transpose_k:
        out_specs.append(pl.BlockSpec((1, D_MODEL, tm),
                                      lambda i, j: (i // tiles_per_batch, 0, i % tiles_per_batch)))
        out_shape.append(jax.ShapeDtypeStruct((batch, D_MODEL, seq), F32))
    res = pl.pallas_call(
        functools.partial(_in_proj_kernel, transpose_k=transpose_k),
        grid=(n // tm, N_GROUPS),
        in_specs=[
            pl.BlockSpec((tm, D_MODEL), lambda i, j: (i, 0)),
            pl.BlockSpec((1, D_MODEL), lambda i, j: (0, 0)),
            pl.BlockSpec((D_MODEL, D_MODEL), lambda i, j: (0, j)),
        ],
        out_specs=out_specs,
        out_shape=out_shape,
        scratch_shapes=[pltpu.VMEM((tm, D_MODEL), BF16)],
        compiler_params=_compiler_params(("arbitrary", "arbitrary")),
        name="in_proj",
    )(x, gain, w_in)
    return (res[0], res[1]) if transpose_k else (res[0], None)


def _hgrn_levels(chunk):
    levels = []
    b = chunk // 2
    while b >= HGRN_SUB:
        levels.append(b)
        b //= 2
    return levels


def _hgrn_kernel(hq_ref, f_ref, v_ref, sg_ref, lbl_ref, wg_ref, s0_ref, o_ref, sfin_ref, st_scr,
                 *, chunk, n_chunks, has_s0):
    t_blk = pl.program_id(1)
    hd = HEAD_DIM

    @pl.when(t_blk == 0)
    def _():
        for h in range(N_HEADS):
            if has_s0:
                st_scr[h] = s0_ref[0, h].T
            else:
                st_scr[h] = jnp.zeros((hd, hd), F32)

    lbl = lbl_ref[...]
    lmax = jnp.max(lbl, axis=0, keepdims=True)
    lexp = jnp.exp(lbl - lmax)
    lb_all = lexp[0:1, :] / jnp.sum(lexp, axis=0, keepdims=True)

    levels = _hgrn_levels(chunk)
    row = lax.broadcasted_iota(jnp.int32, (chunk, hd), 0)
    tril = (lax.broadcasted_iota(jnp.int32, (chunk, chunk), 0)
            >= lax.broadcasted_iota(jnp.int32, (chunk, chunk), 1)).astype(F32)
    rr = lax.broadcasted_iota(jnp.int32, (chunk, chunk), 0)
    cc = lax.broadcasted_iota(jnp.int32, (chunk, chunk), 1)
    is_right = {b: (row & b) != 0 for b in levels}
    same_pair = {b: (rr // (2 * b)) == (cc // (2 * b)) for b in levels}
    sub_row = lax.broadcasted_iota(jnp.int32, (HGRN_SUB, hd), 0)
    wg = wg_ref[...]

    def chunk_body(c, carry):
        r0 = pl.multiple_of(c * chunk, chunk)
        rows = pl.ds(r0, chunk)
        for h in range(N_HEADS):
            cols = slice(h * hd, (h + 1) * hd)
            lb = lb_all[:, cols]
            q = hq_ref[0, 0, rows, cols]
            fa = f_ref[0, 0, rows, cols]
            v = v_ref[0, 0, rows, cols]
            f = lb + (1.0 - lb) * _sigmoid(fa)
            g = jnp.log(f)
            k = 1.0 - f
            cum = jnp.dot(tril, g, preferred_element_type=F32, precision=lax.Precision.HIGHEST)
            v_b = v.astype(BF16)

            o = jnp.zeros((chunk, hd), F32)
            if levels:
                a = jnp.zeros((chunk, chunk), F32)
                for b in levels:
                    pieces = []
                    for p in range(chunk // (2 * b)):
                        ref_row = cum[p * 2 * b + b - 1:p * 2 * b + b, :]
                        pieces.append(jnp.broadcast_to(ref_row, (2 * b, hd)))
                    ref = pieces[0] if len(pieces) == 1 else jnp.concatenate(pieces, axis=0)
                    e = jnp.exp(-jnp.abs(cum - ref))
                    ql = (q * jnp.where(is_right[b], e, 0.0)).astype(BF16)
                    kl = (k * jnp.where(is_right[b], 0.0, e)).astype(BF16)
                    al = lax.dot_general(ql, kl, (((1,), (1,)), ((), ())), preferred_element_type=F32)
                    a = a + jnp.where(same_pair[b], al, 0.0)
                o = jnp.dot(a.astype(BF16), v_b, preferred_element_type=F32)

            diag = []
            for j in range(chunk // HGRN_SUB):
                sl = slice(j * HGRN_SUB, (j + 1) * HGRN_SUB)
                qj, kj, cj, vj = q[sl], k[sl], cum[sl], v[sl]
                od = jnp.zeros((HGRN_SUB, hd), F32)
                for s in range(HGRN_SUB):
                    dec = jnp.exp(jnp.minimum(cj - cj[s:s + 1, :], 0.0))
                    dec = jnp.where(sub_row >= s, dec, 0.0)
                    a_s = jnp.sum(qj * dec * kj[s:s + 1, :], axis=-1, keepdims=True)
                    od = od + a_s * vj[s:s + 1, :]
                diag.append(od)
            o = o + (diag[0] if len(diag) == 1 else jnp.concatenate(diag, axis=0))

            st = st_scr[h]
            q0 = (q * jnp.exp(cum)).astype(BF16)
            o = o + lax.dot_general(q0, st.astype(BF16), (((1,), (1,)), ((), ())),
                                    preferred_element_type=F32)
            last = cum[chunk - 1:chunk, :]
            kd = (k * jnp.exp(last - cum)).astype(BF16)
            upd = jnp.dot(v.T.astype(BF16), kd, preferred_element_type=F32)
            st_scr[h] = st * jnp.exp(last) + upd

            o_ref[0, rows, cols] = _rms(o, wg) * sg_ref[0, 0, rows, cols]
        return carry

    lax.fori_loop(0, n_chunks, chunk_body, 0)

    @pl.when(t_blk == pl.num_programs(1) - 1)
    def _():
        for h in range(N_HEADS):
            sfin_ref[0, h] = st_scr[h].T


def _hgrn(proj, lb_logits, w_gnorm, s0, batch, seq):
    chunk = math.gcd(seq, HGRN_CHUNK)
    assert chunk % HGRN_SUB == 0
    tb = math.gcd(seq, 512)
    n_tb = seq // tb
    proj4 = proj.reshape(N_GROUPS, batch, seq, D_MODEL)
    has_s0 = s0 is not None
    if not has_s0:
        s0 = jnp.zeros((1, N_HEADS, HEAD_DIM, HEAD_DIM), F32)
    s0_map = (lambda b, t: (b, 0, 0, 0)) if has_s0 else (lambda b, t: (0, 0, 0, 0))
    n_lb = lb_logits.shape[0]

    def group(g):
        return pl.BlockSpec((1, 1, tb, D_MODEL), lambda b, t: (g, b, t, 0))

    o, s_fin = pl.pallas_call(
        functools.partial(_hgrn_kernel, chunk=chunk, n_chunks=tb // chunk, has_s0=has_s0),
        grid=(batch, n_tb),
        in_specs=[
            group(COL_HQ), group(COL_F), group(COL_I), group(COL_G),
            pl.BlockSpec((n_lb, D_MODEL), lambda b, t: (0, 0)),
            pl.BlockSpec((1, HEAD_DIM), lambda b, t: (0, 0)),
            pl.BlockSpec((1, N_HEADS, HEAD_DIM, HEAD_DIM), s0_map),
        ],
        out_specs=[
            pl.BlockSpec((1, tb, D_MODEL), lambda b, t: (b, t, 0)),
            pl.BlockSpec((1, N_HEADS, HEAD_DIM, HEAD_DIM), lambda b, t: (b, 0, 0, 0)),
        ],
        out_shape=[
            jax.ShapeDtypeStruct((batch, seq, D_MODEL), F32),
            jax.ShapeDtypeStruct((batch, N_HEADS, HEAD_DIM, HEAD_DIM), F32),
        ],
        scratch_shapes=[pltpu.VMEM((N_HEADS, HEAD_DIM, HEAD_DIM), F32)],
        compiler_params=_compiler_params(("arbitrary", "arbitrary")),
        name="hgrn2",
    )(proj4, proj4, proj4, proj4, lb_logits, w_gnorm, s0)
    return o.reshape(batch * seq, D_MODEL), s_fin


def _lambda_value(lam_ref, lam_init):
    l = lam_ref[...]
    a = jnp.sum(l[0:1, :] * l[1:2, :], axis=-1, keepdims=True)
    b = jnp.sum(l[2:3, :] * l[3:4, :], axis=-1, keepdims=True)
    return jnp.exp(a) - jnp.exp(b) + lam_init


def _split_maps(qh):
    lane = lax.broadcasted_iota(jnp.int32, qh.shape, 1)
    return jnp.concatenate([jnp.where(lane < QK_DIM, qh, 0.0), jnp.where(lane >= QK_DIM, qh, 0.0)], axis=0)


def _prompt_attn_kernel(q_ref, kt_ref, v_ref, lam_ref, wsub_ref, o_ref, ktb_scr, vb_scr, acc_scr,
                        *, tq, n_q, lam_init):
    ktb_scr[...] = kt_ref[0].astype(BF16)
    vb_scr[...] = v_ref[0, 0].astype(BF16)
    lam = _lambda_value(lam_ref, lam_init)
    wsub = wsub_ref[...]
    scale = QK_DIM ** -0.5 * LOG2E
    tri = (lax.broadcasted_iota(jnp.int32, (2 * tq, tq), 1)
           <= (lax.broadcasted_iota(jnp.int32, (2 * tq, tq), 0) % tq))

    def q_body(qi, carry):
        q0 = pl.multiple_of(qi * tq, tq)
        qq = _split_maps(q_ref[0, 0, pl.ds(q0, tq), :] * scale).astype(BF16)
        acc_scr[...] = jnp.zeros((2 * tq, HEAD_DIM), F32)

        def kv_step(j, ml, masked):
            m, l = ml
            k0 = pl.multiple_of(j * tq, tq)
            s = jnp.dot(qq, ktb_scr[:, pl.ds(k0, tq)], preferred_element_type=F32)
            if masked:
                s = jnp.where(tri, s, -jnp.inf)
            m_new = jnp.maximum(m, jnp.max(s, axis=-1, keepdims=True))
            alpha = jnp.exp2(m - m_new)
            p = jnp.exp2(s - m_new)
            l_new = alpha * l + jnp.sum(p, axis=-1, keepdims=True)
            acc_scr[...] = alpha * acc_scr[...] + jnp.dot(p.astype(BF16), vb_scr[pl.ds(k0, tq), :],
                                                          preferred_element_type=F32)
            return m_new, l_new

        init = (jnp.full((2 * tq, 1), -jnp.inf, F32), jnp.zeros((2 * tq, 1), F32))
        ml = lax.fori_loop(0, qi, lambda j, ml: kv_step(j, ml, False), init)
        m, l = kv_step(qi, ml, True)
        on = acc_scr[...] / l
        o = on[0:tq, :] - lam * on[tq:2 * tq, :]
        o_ref[0, pl.ds(q0, tq), :] = _rms(o, wsub) * (1.0 - lam_init)
        return carry

    lax.fori_loop(0, n_q, q_body, 0)


def _prompt_attn(proj, kt, lam_params, w_subln, batch, seq, lam_init):
    tq = math.gcd(seq, 512)
    proj4 = proj.reshape(N_GROUPS, batch, seq, D_MODEL)
    o = pl.pallas_call(
        functools.partial(_prompt_attn_kernel, tq=tq, n_q=seq // tq, lam_init=lam_init),
        grid=(batch, N_HEADS),
        in_specs=[pl.BlockSpec((1, 1, seq, HEAD_DIM), lambda b, h: (COL_QB, b, 0, h)),
                  pl.BlockSpec((1, HEAD_DIM, seq), lambda b, h: (b, h, 0)),
                  pl.BlockSpec((1, 1, seq, HEAD_DIM), lambda b, h: (COL_VB, b, 0, h)),
                  pl.BlockSpec((4, QK_DIM), lambda b, h: (0, 0)),
                  pl.BlockSpec((1, HEAD_DIM), lambda b, h: (0, 0))],
        out_specs=pl.BlockSpec((1, seq, HEAD_DIM), lambda b, h: (b, 0, h)),
        out_shape=jax.ShapeDtypeStruct((batch, seq, D_MODEL), F32),
        scratch_shapes=[pltpu.VMEM((HEAD_DIM, seq), BF16), pltpu.VMEM((seq, HEAD_DIM), BF16),
                        pltpu.VMEM((2 * tq, HEAD_DIM), F32)],
        compiler_params=_compiler_params(("arbitrary", "arbitrary")),
        name="diff_attn_prompt",
    )(proj4, kt, proj4, lam_params, w_subln)
    return o.reshape(batch * seq, D_MODEL)


def _sample_attn_kernel(pt_ref, q_ref, kn_ref, vn_ref, lam_ref, wsub_ref, *rest,
                        pages_per_step, page_size, n_new, lam_init):
    del pt_ref
    k_pages = rest[:pages_per_step]
    v_pages = rest[pages_per_step:2 * pages_per_step]
    o_ref, q_scr, m_scr, l_scr, acc_scr = rest[2 * pages_per_step:]
    step = pl.program_id(1)
    nrow = 2 * n_new
    hd = HEAD_DIM

    @pl.when(step == 0)
    def _():
        q = q_ref[0, 0] * (QK_DIM ** -0.5 * LOG2E)
        for h in range(N_HEADS):
            q_scr[h * nrow:(h + 1) * nrow, :] = _split_maps(q[:, h * hd:(h + 1) * hd])
        m_scr[...] = jnp.full(m_scr.shape, -jnp.inf, F32)
        l_scr[...] = jnp.zeros(l_scr.shape, F32)
        acc_scr[...] = jnp.zeros(acc_scr.shape, F32)

    def update(scores, pv_fn):
        s = jnp.concatenate(scores, axis=0)
        m_old = m_scr[...]
        m_new = jnp.maximum(m_old, jnp.max(s, axis=-1, keepdims=True))
        alpha = jnp.exp2(m_old - m_new)
        p = jnp.exp2(s - m_new)
        l_new = alpha * l_scr[...] + jnp.sum(p, axis=-1, keepdims=True)
        pv = jnp.concatenate([pv_fn(h, p[h * nrow:(h + 1) * nrow, :]) for h in range(N_HEADS)], axis=0)
        acc_new = alpha * acc_scr[...] + pv
        m_scr[...] = m_new
        l_scr[...] = l_new
        acc_scr[...] = acc_new
        return l_new, acc_new

    def head_keys(h):
        kt = [kp[h * hd:(h + 1) * hd, :].astype(BF16) for kp in k_pages]
        return kt[0] if len(kt) == 1 else jnp.concatenate(kt, axis=1)

    def head_values(h):
        vv = [vp[pl.ds(h, page_size, stride=N_HEADS), :].astype(BF16) for vp in v_pages]
        return vv[0] if len(vv) == 1 else jnp.concatenate(vv, axis=0)

    q_all = q_scr[...].astype(BF16)
    update([jnp.dot(q_all[h * nrow:(h + 1) * nrow, :], head_keys(h), preferred_element_type=F32)
            for h in range(N_HEADS)],
           lambda h, p: jnp.dot(p.astype(BF16), head_values(h), preferred_element_type=F32))

    @pl.when(step == pl.num_programs(1) - 1)
    def _():
        key = lax.broadcasted_iota(jnp.int32, (nrow, n_new), 1)
        qry = lax.broadcasted_iota(jnp.int32, (nrow, n_new), 0) % n_new
        q_f32 = q_scr[...]
        scores = []
        for h in range(N_HEADS):
            s = lax.dot_general(q_f32[h * nrow:(h + 1) * nrow, :], kn_ref[0, 0, :, h * hd:(h + 1) * hd],
                                (((1,), (1,)), ((), ())), preferred_element_type=F32)
            scores.append(jnp.where(key <= qry, s, -jnp.inf))
        l_fin, acc_fin = update(
            scores, lambda h, p: jnp.dot(p, vn_ref[0, 0, :, h * hd:(h + 1) * hd], preferred_element_type=F32))
        on = acc_fin / l_fin
        lam = _lambda_value(lam_ref, lam_init)
        wsub = wsub_ref[...]
        for h in range(N_HEADS):
            o = on[h * nrow:h * nrow + n_new, :] - lam * on[h * nrow + n_new:(h + 1) * nrow, :]
            o_ref[0, :, h * hd:(h + 1) * hd] = _rms(o, wsub) * (1.0 - lam_init)


def _sample_attn(proj, cache_k, cache_v, page_table, lam_params, w_subln, n_seq, n_new, lam_init):
    n_pages = page_table.shape[1]
    n_pool, page_size = cache_k.shape[0], cache_k.shape[1]
    pages_per_step = math.gcd(n_pages, 4)
    assert n_new % 8 == 0
    ckt = jnp.transpose(cache_k, (0, 2, 3, 4, 1)).reshape(n_pool, D_MODEL, page_size)
    cvr = cache_v.reshape(n_pool, page_size * N_HEADS, HEAD_DIM)
    proj4 = proj.reshape(N_GROUPS, n_seq, n_new, D_MODEL)

    def seq_blk(g):
        return pl.BlockSpec((1, 1, n_new, D_MODEL), lambda n, s, pt: (g, n, 0, 0))

    def k_spec(i):
        return pl.BlockSpec((None, D_MODEL, page_size),
                            lambda n, s, pt: (pt[n, s * pages_per_step + i], 0, 0))

    def v_spec(i):
        return pl.BlockSpec((None, page_size * N_HEADS, HEAD_DIM),
                            lambda n, s, pt: (pt[n, s * pages_per_step + i], 0, 0))

    nrows = N_HEADS * 2 * n_new
    grid_spec = pltpu.PrefetchScalarGridSpec(
        num_scalar_prefetch=1,
        grid=(n_seq, n_pages // pages_per_step),
        in_specs=[seq_blk(COL_QB), seq_blk(COL_KB), seq_blk(COL_VB),
                  pl.BlockSpec((4, QK_DIM), lambda n, s, pt: (0, 0)),
                  pl.BlockSpec((1, HEAD_DIM), lambda n, s, pt: (0, 0))]
                 + [k_spec(i) for i in range(pages_per_step)]
                 + [v_spec(i) for i in range(pages_per_step)],
        out_specs=pl.BlockSpec((1, n_new, D_MODEL), lambda n, s, pt: (n, 0, 0)),
        scratch_shapes=[pltpu.VMEM((nrows, HEAD_DIM), F32), pltpu.VMEM((nrows, 1), F32),
                        pltpu.VMEM((nrows, 1), F32), pltpu.VMEM((nrows, HEAD_DIM), F32)],
    )
    o = pl.pallas_call(
        functools.partial(_sample_attn_kernel, pages_per_step=pages_per_step, page_size=page_size,
                          n_new=n_new, lam_init=lam_init),
        grid_spec=grid_spec,
        out_shape=jax.ShapeDtypeStruct((n_seq, n_new, D_MODEL), F32),
        compiler_params=_compiler_params(("arbitrary", "arbitrary")),
        name="diff_attn_sample",
    )(page_table, proj4, proj4, proj4, lam_params, w_subln,
      *([ckt] * pages_per_step), *([cvr] * pages_per_step))
    return o.reshape(n_seq * n_new, D_MODEL)


def _merge_kernel(oa_ref, ob_ref, ga_ref, gb_ref, x_ref, wa_ref, wb_ref, wo_ref, g_ref, h_ref):
    ya = jnp.dot(oa_ref[...].astype(BF16), wa_ref[...], preferred_element_type=F32)
    yb = jnp.dot(ob_ref[...].astype(BF16), wb_ref[...], preferred_element_type=F32)
    y = ga_ref[0] * ya + gb_ref[0] * yb
    mix = jnp.dot(y.astype(BF16), wo_ref[...], preferred_element_type=F32)
    h_ref[...] = x_ref[...] + _rms(mix, g_ref[...])


def _merge(oa, ob, proj, x, wa, wb, wo, g_post, tm):
    n = x.shape[0]
    row = pl.BlockSpec((tm, D_MODEL), lambda i: (i, 0))
    wspec = pl.BlockSpec((D_MODEL, D_MODEL), lambda i: (0, 0))
    return pl.pallas_call(
        _merge_kernel,
        grid=(n // tm,),
        in_specs=[row, row,
                  pl.BlockSpec((1, tm, D_MODEL), lambda i: (COL_GA, i, 0)),
                  pl.BlockSpec((1, tm, D_MODEL), lambda i: (COL_GB, i, 0)),
                  row, wspec, wspec, wspec, pl.BlockSpec((1, D_MODEL), lambda i: (0, 0))],
        out_specs=row,
        out_shape=jax.ShapeDtypeStruct((n, D_MODEL), F32),
        compiler_params=_compiler_params(("arbitrary",)),
        name="merge_out",
    )(oa, ob, proj, proj, x, wa, wb, wo, g_post)


def _mem_attn_kernel(h_ref, mk_ref, mv_ref, gpre_ref, wq_ref, wo_ref, gpost_ref, o_ref, *, per_head_layout):
    h = h_ref[...]
    q = jnp.dot(_rms(h, gpre_ref[...]).astype(BF16), wq_ref[...], preferred_element_type=F32)
    q = (q * (MEM_DIM ** -0.5)).astype(BF16)
    outs = []
    for hh in range(MEM_HEADS):
        cols = slice(hh * MEM_DIM, (hh + 1) * MEM_DIM)
        if per_head_layout:
            mk, mv = mk_ref[0, :, hh, :], mv_ref[0, :, hh, :]
        else:
            mk, mv = mk_ref[0, :, cols], mv_ref[0, :, cols]
        s = lax.dot_general(q[:, cols], mk.astype(BF16), (((1,), (1,)), ((), ())),
                            preferred_element_type=F32)
        p = jnp.exp(s - jnp.max(s, axis=-1, keepdims=True))
        p = p / jnp.sum(p, axis=-1, keepdims=True)
        outs.append(jnp.dot(p.astype(BF16), mv.astype(BF16), preferred_element_type=F32))
    o = jnp.concatenate(outs, axis=-1).astype(BF16)
    mo = jnp.dot(o, wo_ref[...], preferred_element_type=F32)
    o_ref[...] = h + _rms(mo, gpost_ref[...])


def _mem_attn(h, mk, mv, g_pre, wq, wo, g_post, rows_per_batch, tm):
    n = h.shape[0]
    tiles_per_batch = rows_per_batch // tm
    per_head_layout = mk.ndim == 4
    row = pl.BlockSpec((tm, D_MODEL), lambda i: (i, 0))
    if per_head_layout:
        mem = pl.BlockSpec((1,) + mk.shape[1:], lambda i: (i // tiles_per_batch, 0, 0, 0))
    else:
        mem = pl.BlockSpec((1,) + mk.shape[1:], lambda i: (i // tiles_per_batch, 0, 0))
    wspec = pl.BlockSpec((D_MODEL, D_MODEL), lambda i: (0, 0))
    gspec = pl.BlockSpec((1, D_MODEL), lambda i: (0, 0))
    return pl.pallas_call(
        functools.partial(_mem_attn_kernel, per_head_layout=per_head_layout),
        grid=(n // tm,),
        in_specs=[row, mem, mem, gspec, wspec, wspec, gspec],
        out_specs=row,
        out_shape=jax.ShapeDtypeStruct((n, D_MODEL), F32),
        compiler_params=_compiler_params(("arbitrary",)),
        name="mem_attn",
    )(h, mk, mv, g_pre, wq, wo, g_post)


def _mlp_kernel(h_ref, gpre_ref, wup_ref, wdown_ref, gpost_ref, o_ref, xn_scr, acc_scr):
    j = pl.program_id(1)

    @pl.when(j == 0)
    def _():
        xn_scr[...] = _rms(h_ref[...], gpre_ref[...]).astype(BF16)
        acc_scr[...] = jnp.zeros_like(acc_scr)

    u = jnp.dot(xn_scr[...], wup_ref[...], preferred_element_type=F32)
    u = jnp.square(jnp.maximum(u, 0.0)).astype(BF16)
    acc_scr[...] += jnp.dot(u, wdown_ref[...], preferred_element_type=F32)

    @pl.when(j == pl.num_programs(1) - 1)
    def _():
        o_ref[...] = h_ref[...] + _rms(acc_scr[...], gpost_ref[...])


def _mlp(h, g_pre, wup, wdown, g_post, tm):
    n = h.shape[0]
    tf = D_MODEL
    row = pl.BlockSpec((tm, D_MODEL), lambda i, j: (i, 0))
    gspec = pl.BlockSpec((1, D_MODEL), lambda i, j: (0, 0))
    return pl.pallas_call(
        _mlp_kernel,
        grid=(n // tm, D_FF // tf),
        in_specs=[row, gspec,
                  pl.BlockSpec((D_MODEL, tf), lambda i, j: (0, j)),
                  pl.BlockSpec((tf, D_MODEL), lambda i, j: (j, 0)),
                  gspec],
        out_specs=row,
        out_shape=jax.ShapeDtypeStruct((n, D_MODEL), F32),
        scratch_shapes=[pltpu.VMEM((tm, D_MODEL), BF16), pltpu.VMEM((tm, D_MODEL), F32)],
        compiler_params=_compiler_params(("arbitrary", "arbitrary")),
        name="mlp",
    )(h, g_pre, wup, wdown, g_post)


def _layer(x, batch, seq, s0, past, mem_kv, p, lam_init):
    n = x.shape[0]
    tm = math.gcd(n, 512)
    proj, kt = _in_proj(x, p["w_pre_mix"], p["w_in"], batch, seq, transpose_k=past is None)
    o_a, s_new = _hgrn(proj, p["lb_logits"], p["w_hgrn_gnorm"], s0, batch, seq)
    if past is None:
        o_b = _prompt_attn(proj, kt, p["lam_params"], p["w_subln"], batch, seq, lam_init)
    else:
        o_b = _sample_attn(proj, past[0], past[1], past[2], p["lam_params"], p["w_subln"],
                           batch, seq, lam_init)
    h = _merge(o_a, o_b, proj, x, p["w_branch_a"], p["w_branch_b"], p["w_out"], p["w_post_mix"], tm)
    mk, mv = mem_kv
    h = _mem_attn(h, mk, mv, p["w_pre_mem"], p["w_mq"], p["w_mo"], p["w_post_mem"], seq,
                  math.gcd(seq, 512))
    h = _mlp(h, p["w_pre_mlp"], p["w_up"], p["w_down"], p["w_post_mlp"], tm)
    return h, s_new, proj, kt


def kernel(x_prompt, x_sample, cache_attn_k, cache_attn_v, state_hgrn, cache_mem_k, cache_mem_v,
           page_table, mem_prompt, w_pre_mix, w_in, hgrn_lb_logits, w_hgrn_gnorm,
           lambda_q1, lambda_k1, lambda_q2, lambda_k2, w_subln, w_branch_a, w_branch_b, w_out,
           w_post_mix, w_pre_mem, w_mem_norm, w_mq, w_mk, w_mv, w_mo, w_post_mem,
           w_pre_mlp, w_up, w_down, w_post_mlp):
    depth = w_in.shape[0]
    assert depth == 1, "single-layer step"
    l = 0
    batch, seq, _ = x_prompt.shape
    n_seq, n_new, _ = x_sample.shape
    mem_len = mem_prompt.shape[1]
    lam_init = 0.8 - 0.6 * math.exp(-0.3 * l)

    def gain(w):
        return w[l].reshape(1, -1)

    def weight(w):
        return w[l].astype(BF16)

    p = {
        "w_pre_mix": gain(w_pre_mix), "w_in": weight(w_in), "lb_logits": hgrn_lb_logits,
        "w_hgrn_gnorm": gain(w_hgrn_gnorm), "w_subln": gain(w_subln),
        "lam_params": jnp.stack([lambda_q1[l], lambda_k1[l], lambda_q2[l], lambda_k2[l]]),
        "w_branch_a": weight(w_branch_a), "w_branch_b": weight(w_branch_b), "w_out": weight(w_out),
        "w_post_mix": gain(w_post_mix), "w_pre_mem": gain(w_pre_mem), "w_mq": weight(w_mq),
        "w_mo": weight(w_mo), "w_post_mem": gain(w_post_mem), "w_pre_mlp": gain(w_pre_mlp),
        "w_up": weight(w_up), "w_down": weight(w_down), "w_post_mlp": gain(w_post_mlp),
    }

    memx = mem_prompt.reshape(batch * mem_len, D_MODEL)
    tm_mem = math.gcd(batch * mem_len, 512)
    mk_p = _norm_matmul(memx, gain(w_mem_norm), weight(w_mk), tm_mem).reshape(batch, mem_len, D_MODEL)
    mv_p = _norm_matmul(memx, gain(w_mem_norm), weight(w_mv), tm_mem).reshape(batch, mem_len, D_MODEL)
    hp, s_p, proj_p, kt_p = _layer(x_prompt.reshape(batch * seq, D_MODEL), batch, seq, None, None,
                                   (mk_p, mv_p), p, lam_init)
    k_p = jnp.transpose(kt_p.reshape(batch, N_HEADS, 2, QK_DIM, seq), (0, 4, 1, 2, 3))
    v_p = proj_p[COL_VB].reshape(batch, seq, N_HEADS, HEAD_DIM)

    past = (cache_attn_k[l], cache_attn_v[l], page_table)
    hs, s_s, proj_s, _ = _layer(x_sample.reshape(n_seq * n_new, D_MODEL), n_seq, n_new, state_hgrn[l],
                                past, (cache_mem_k[l], cache_mem_v[l]), p, lam_init)
    k_s = proj_s[COL_KB].reshape(n_seq, n_new, N_HEADS, 2, QK_DIM)
    v_s = proj_s[COL_VB].reshape(n_seq, n_new, N_HEADS, HEAD_DIM)

    return (hp.reshape(batch, seq, D_MODEL), hs.reshape(n_seq, n_new, D_MODEL),
            k_p[None], v_p[None], s_p[None],
            mk_p.reshape(1, batch, mem_len, MEM_HEADS, MEM_DIM),
            mv_p.reshape(1, batch, mem_len, MEM_HEADS, MEM_DIM),
            k_s[None], v_s[None], s_s[None])
```

```python
import functools
import math

import jax
import jax.numpy as jnp
from jax import lax
from jax.experimental import pallas as pl
from jax.experimental.pallas import tpu as pltpu

F32 = jnp.float32
BF16 = jnp.bfloat16

D_MODEL = 1024
N_HEADS = 8
HEAD_DIM = 128
QK_DIM = 64
MEM_HEADS = 4
MEM_DIM = 256
D_FF = 4 * D_MODEL
EPS = 1e-6
LOG2E = math.log2(math.e)
HGRN_CHUNK = 64
HGRN_SUB = 8
VMEM_LIMIT_BYTES = 56 * 1024 * 1024

COL_HQ, COL_F, COL_I, COL_G, COL_QB, COL_KB, COL_VB, COL_GA, COL_GB = range(9)
N_GROUPS = 9


def _sigmoid(x):
    return 1.0 / (1.0 + jnp.exp(-x))


def _rms(x, gain):
    ms = jnp.mean(x * x, axis=-1, keepdims=True)
    return x * lax.rsqrt(ms + EPS) * gain


def _compiler_params(semantics):
    return pltpu.CompilerParams(dimension_semantics=semantics, vmem_limit_bytes=VMEM_LIMIT_BYTES)


def _norm_matmul_kernel(x_ref, g_ref, w_ref, o_ref):
    xn = _rms(x_ref[...], g_ref[...]).astype(BF16)
    o_ref[...] = jnp.dot(xn, w_ref[...], preferred_element_type=F32)


def _norm_matmul(x, gain, w, tm):
    n, d = x.shape
    return pl.pallas_call(
        _norm_matmul_kernel,
        grid=(n // tm,),
        in_specs=[
            pl.BlockSpec((tm, d), lambda i: (i, 0)),
            pl.BlockSpec((1, d), lambda i: (0, 0)),
            pl.BlockSpec((d, D_MODEL), lambda i: (0, 0)),
        ],
        out_specs=pl.BlockSpec((tm, D_MODEL), lambda i: (i, 0)),
        out_shape=jax.ShapeDtypeStruct((n, D_MODEL), F32),
        compiler_params=_compiler_params(("arbitrary",)),
        name="norm_matmul",
    )(x, gain, w)


def _in_proj_kernel(x_ref, g_ref, w_ref, o_ref, *rest, transpose_k):
    if transpose_k:
        kt_ref, xn_scr = rest
    else:
        (xn_scr,) = rest
    j = pl.program_id(1)

    @pl.when(j == 0)
    def _():
        xn_scr[...] = _rms(x_ref[...], g_ref[...]).astype(BF16)

    y = jnp.dot(xn_scr[...], w_ref[j], preferred_element_type=F32)
    is_silu = (j == COL_HQ) | (j == COL_G)
    is_sigmoid = (j == COL_GA) | (j == COL_GB)
    sig = 0.5 * jnp.tanh(0.5 * y) + 0.5
    o_ref[0] = jnp.where(is_silu, y * sig, jnp.where(is_sigmoid, sig, y))

    if transpose_k:
        @pl.when(j == COL_KB)
        def _():
            kt_ref[0] = y.T


def _in_proj(x, gain, w_in, batch, seq, transpose_k):
    n = x.shape[0]
    tm = math.gcd(seq, 512) if transpose_k else math.gcd(n, 512)
    tiles_per_batch = seq // tm if transpose_k else 1
    out_specs = [pl.BlockSpec((1, tm, D_MODEL), lambda i, j: (j, i, 0))]
    out_shape = [jax.ShapeDtypeStruct((N_GROUPS, n, D_MODEL), F32)]
    if transpose_k:
        out_specs.append(pl.BlockSpec((1, D_MODEL, tm),
                                      lambda i, j: (i // tiles_per_batch, 0, i % tiles_per_batch)))
        out_shape.append(jax.ShapeDtypeStruct((batch, D_MODEL, seq), F32))
    res = pl.pallas_call(
        functools.partial(_in_proj_kernel, transpose_k=transpose_k),
        grid=(n // tm, N_GROUPS),
        in_specs=[
            pl.BlockSpec((tm, D_MODEL), lambda i, j: (i, 0)),
            pl.BlockSpec((1, D_MODEL), lambda i, j: (0, 0)),
            pl.BlockSpec((N_GROUPS, D_MODEL, D_MODEL), lambda i, j: (0, 0, 0),
                         pipeline_mode=pl.Buffered(1)),
        ],
        out_specs=out_specs,
        out_shape=out_shape,
        scratch_shapes=[pltpu.VMEM((tm, D_MODEL), BF16)],
        compiler_params=_compiler_params(("arbitrary", "arbitrary")),
        name="in_proj",
    )(x, gain, w_in)
    return (res[0], res[1]) if transpose_k else (res[0], None)


def _hgrn_levels(chunk):
    levels = []
    b = chunk // 2
    while b >= HGRN_SUB:
        levels.append(b)
        b //= 2
    return levels


def _hgrn_kernel(hq_ref, f_ref, v_ref, sg_ref, lbl_ref, wg_ref, s0_ref, o_ref, sfin_ref, st_scr,
                 *, chunk, n_chunks, has_s0):
    t_blk = pl.program_id(1)
    hd = HEAD_DIM

    @pl.when(t_blk == 0)
    def _():
        for h in range(N_HEADS):
            if has_s0:
                st_scr[h] = s0_ref[0, h].T
            else:
                st_scr[h] = jnp.zeros((hd, hd), F32)

    lbl = lbl_ref[...]
    lmax = jnp.max(lbl, axis=0, keepdims=True)
    lexp = jnp.exp(lbl - lmax)
    lb_all = lexp[0:1, :] / jnp.sum(lexp, axis=0, keepdims=True)

    levels = _hgrn_levels(chunk)
    row = lax.broadcasted_iota(jnp.int32, (chunk, hd), 0)
    tril = (lax.broadcasted_iota(jnp.int32, (chunk, chunk), 0)
            >= lax.broadcasted_iota(jnp.int32, (chunk, chunk), 1)).astype(F32)
    rr = lax.broadcasted_iota(jnp.int32, (chunk, chunk), 0)
    cc = lax.broadcasted_iota(jnp.int32, (chunk, chunk), 1)
    is_right = {b: (row & b) != 0 for b in levels}
    same_pair = {b: (rr // (2 * b)) == (cc // (2 * b)) for b in levels}
    sub_row = lax.broadcasted_iota(jnp.int32, (HGRN_SUB, hd), 0)
    wg = wg_ref[...]

    def chunk_body(c, carry):
        r0 = pl.multiple_of(c * chunk, chunk)
        rows = pl.ds(r0, chunk)
        for h in range(N_HEADS):
            cols = slice(h * hd, (h + 1) * hd)
            lb = lb_all[:, cols]
            q = hq_ref[0, 0, rows, cols]
            fa = f_ref[0, 0, rows, cols]
            v = v_ref[0, 0, rows, cols]
            f = lb + (1.0 - lb) * _sigmoid(fa)
            g = jnp.log(f)
            k = 1.0 - f
            cum = jnp.dot(tril, g, preferred_element_type=F32, precision=lax.Precision.HIGHEST)
            v_b = v.astype(BF16)

            o = jnp.zeros((chunk, hd), F32)
            if levels:
                a = jnp.zeros((chunk, chunk), F32)
                for b in levels:
                    pieces = []
                    for p in range(chunk // (2 * b)):
                        ref_row = cum[p * 2 * b + b - 1:p * 2 * b + b, :]
                        pieces.append(jnp.broadcast_to(ref_row, (2 * b, hd)))
                    ref = pieces[0] if len(pieces) == 1 else jnp.concatenate(pieces, axis=0)
                    e = jnp.exp(-jnp.abs(cum - ref))
                    ql = (q * jnp.where(is_right[b], e, 0.0)).astype(BF16)
                    kl = (k * jnp.where(is_right[b], 0.0, e)).astype(BF16)
                    al = lax.dot_general(ql, kl, (((1,), (1,)), ((), ())), preferred_element_type=F32)
                    a = a + jnp.where(same_pair[b], al, 0.0)
                o = jnp.dot(a.astype(BF16), v_b, preferred_element_type=F32)

            diag = []
            for j in range(chunk // HGRN_SUB):
                sl = slice(j * HGRN_SUB, (j + 1) * HGRN_SUB)
                qj, kj, cj, vj = q[sl], k[sl], cum[sl], v[sl]
                od = jnp.zeros((HGRN_SUB, hd), F32)
                for s in range(HGRN_SUB):
                    dec = jnp.exp(jnp.minimum(cj - cj[s:s + 1, :], 0.0))
                    dec = jnp.where(sub_row >= s, dec, 0.0)
                    a_s = jnp.sum(qj * dec * kj[s:s + 1, :], axis=-1, keepdims=True)
                    od = od + a_s * vj[s:s + 1, :]
                diag.append(od)
            o = o + (diag[0] if len(diag) == 1 else jnp.concatenate(diag, axis=0))

            st = st_scr[h]
            q0 = (q * jnp.exp(cum)).astype(BF16)
            o = o + lax.dot_general(q0, st.astype(BF16), (((1,), (1,)), ((), ())),
                                    preferred_element_type=F32)
            last = cum[chunk - 1:chunk, :]
            kd = (k * jnp.exp(last - cum)).astype(BF16)
            upd = jnp.dot(v.T.astype(BF16), kd, preferred_element_type=F32)
            st_scr[h] = st * jnp.exp(last) + upd

            o_ref[0, rows, cols] = _rms(o, wg) * sg_ref[0, 0, rows, cols]
        return carry

    lax.fori_loop(0, n_chunks, chunk_body, 0)

    @pl.when(t_blk == pl.num_programs(1) - 1)
    def _():
        for h in range(N_HEADS):
            sfin_ref[0, h] = st_scr[h].T


def _hgrn(proj, lb_logits, w_gnorm, s0, batch, seq):
    chunk = math.gcd(seq, HGRN_CHUNK)
    assert chunk % HGRN_SUB == 0
    tb = math.gcd(seq, 512)
    n_tb = seq // tb
    proj4 = proj.reshape(N_GROUPS, batch, seq, D_MODEL)
    has_s0 = s0 is not None
    if not has_s0:
        s0 = jnp.zeros((1, N_HEADS, HEAD_DIM, HEAD_DIM), F32)
    s0_map = (lambda b, t: (b, 0, 0, 0)) if has_s0 else (lambda b, t: (0, 0, 0, 0))
    n_lb = lb_logits.shape[0]

    def group(g):
        return pl.BlockSpec((1, 1, tb, D_MODEL), lambda b, t: (g, b, t, 0))

    o, s_fin = pl.pallas_call(
        functools.partial(_hgrn_kernel, chunk=chunk, n_chunks=tb // chunk, has_s0=has_s0),
        grid=(batch, n_tb),
        in_specs=[
            group(COL_HQ), group(COL_F), group(COL_I), group(COL_G),
            pl.BlockSpec((n_lb, D_MODEL), lambda b, t: (0, 0)),
            pl.BlockSpec((1, HEAD_DIM), lambda b, t: (0, 0)),
            pl.BlockSpec((1, N_HEADS, HEAD_DIM, HEAD_DIM), s0_map),
        ],
        out_specs=[
            pl.BlockSpec((1, tb, D_MODEL), lambda b, t: (b, t, 0)),
            pl.BlockSpec((1, N_HEADS, HEAD_DIM, HEAD_DIM), lambda b, t: (b, 0, 0, 0)),
        ],
        out_shape=[
            jax.ShapeDtypeStruct((batch, seq, D_MODEL), F32),
            jax.ShapeDtypeStruct((batch, N_HEADS, HEAD_DIM, HEAD_DIM), F32),
        ],
        scratch_shapes=[pltpu.VMEM((N_HEADS, HEAD_DIM, HEAD_DIM), F32)],
        compiler_params=_compiler_params(("arbitrary", "arbitrary")),
        name="hgrn2",
    )(proj4, proj4, proj4, proj4, lb_logits, w_gnorm, s0)
    return o.reshape(batch * seq, D_MODEL), s_fin


def _lambda_value(lam_ref, lam_init):
    l = lam_ref[...]
    a = jnp.sum(l[0:1, :] * l[1:2, :], axis=-1, keepdims=True)
    b = jnp.sum(l[2:3, :] * l[3:4, :], axis=-1, keepdims=True)
    return jnp.exp(a) - jnp.exp(b) + lam_init


def _split_maps(qh):
    lane = lax.broadcasted_iota(jnp.int32, qh.shape, 1)
    return jnp.concatenate([jnp.where(lane < QK_DIM, qh, 0.0), jnp.where(lane >= QK_DIM, qh, 0.0)], axis=0)


def _prompt_attn_kernel(q_ref, kt_ref, v_ref, lam_ref, wsub_ref, o_ref, ktb_scr, vb_scr, acc_scr,
                        *, tq, n_q, lam_init):
    ktb_scr[...] = kt_ref[0].astype(BF16)
    vb_scr[...] = v_ref[0, 0].astype(BF16)
    lam = _lambda_value(lam_ref, lam_init)
    wsub = wsub_ref[...]
    scale = QK_DIM ** -0.5 * LOG2E
    tri = (lax.broadcasted_iota(jnp.int32, (2 * tq, tq), 1)
           <= (lax.broadcasted_iota(jnp.int32, (2 * tq, tq), 0) % tq))

    def q_body(qi, carry):
        q0 = pl.multiple_of(qi * tq, tq)
        qq = _split_maps(q_ref[0, 0, pl.ds(q0, tq), :] * scale).astype(BF16)
        acc_scr[...] = jnp.zeros((2 * tq, HEAD_DIM), F32)

        def kv_step(j, ml, masked):
            m, l = ml
            k0 = pl.multiple_of(j * tq, tq)
            s = jnp.dot(qq, ktb_scr[:, pl.ds(k0, tq)], preferred_element_type=F32)
            if masked:
                s = jnp.where(tri, s, -jnp.inf)
            m_new = jnp.maximum(m, jnp.max(s, axis=-1, keepdims=True))
            alpha = jnp.exp2(m - m_new)
            p = jnp.exp2(s - m_new)
            l_new = alpha * l + jnp.sum(p, axis=-1, keepdims=True)
            acc_scr[...] = alpha * acc_scr[...] + jnp.dot(p.astype(BF16), vb_scr[pl.ds(k0, tq), :],
                                                          preferred_element_type=F32)
            return m_new, l_new

        init = (jnp.full((2 * tq, 1), -jnp.inf, F32), jnp.zeros((2 * tq, 1), F32))
        ml = lax.fori_loop(0, qi, lambda j, ml: kv_step(j, ml, False), init)
        m, l = kv_step(qi, ml, True)
        on = acc_scr[...] / l
        o = on[0:tq, :] - lam * on[tq:2 * tq, :]
        o_ref[0, pl.ds(q0, tq), :] = _rms(o, wsub) * (1.0 - lam_init)
        return carry

    lax.fori_loop(0, n_q, q_body, 0)


def _prompt_attn(proj, kt, lam_params, w_subln, batch, seq, lam_init):
    tq = math.gcd(seq, 512)
    proj4 = proj.reshape(N_GROUPS, batch, seq, D_MODEL)
    o = pl.pallas_call(
        functools.partial(_prompt_attn_kernel, tq=tq, n_q=seq // tq, lam_init=lam_init),
        grid=(batch, N_HEADS),
        in_specs=[pl.BlockSpec((1, 1, seq, HEAD_DIM), lambda b, h: (COL_QB, b, 0, h)),
                  pl.BlockSpec((1, HEAD_DIM, seq), lambda b, h: (b, h, 0)),
                  pl.BlockSpec((1, 1, seq, HEAD_DIM), lambda b, h: (COL_VB, b, 0, h)),
                  pl.BlockSpec((4, QK_DIM), lambda b, h: (0, 0)),
                  pl.BlockSpec((1, HEAD_DIM), lambda b, h: (0, 0))],
        out_specs=pl.BlockSpec((1, seq, HEAD_DIM), lambda b, h: (b, 0, h)),
        out_shape=jax.ShapeDtypeStruct((batch, seq, D_MODEL), F32),
        scratch_shapes=[pltpu.VMEM((HEAD_DIM, seq), BF16), pltpu.VMEM((seq, HEAD_DIM), BF16),
                        pltpu.VMEM((2 * tq, HEAD_DIM), F32)],
        compiler_params=_compiler_params(("arbitrary", "arbitrary")),
        name="diff_attn_prompt",
    )(proj4, kt, proj4, lam_params, w_subln)
    return o.reshape(batch * seq, D_MODEL)


def _sample_attn_kernel(pt_ref, q_ref, kn_ref, vn_ref, lam_ref, wsub_ref, *rest,
                        pages_per_step, page_size, n_new, lam_init):
    del pt_ref
    k_pages = rest[:pages_per_step]
    v_pages = rest[pages_per_step:2 * pages_per_step]
    o_ref, q_scr, m_scr, l_scr, acc_scr = rest[2 * pages_per_step:]
    step = pl.program_id(1)
    nrow = 2 * n_new
    hd = HEAD_DIM

    @pl.when(step == 0)
    def _():
        q = q_ref[0, 0] * (QK_DIM ** -0.5 * LOG2E)
        for h in range(N_HEADS):
            q_scr[h * nrow:(h + 1) * nrow, :] = _split_maps(q[:, h * hd:(h + 1) * hd])
        m_scr[...] = jnp.full(m_scr.shape, -jnp.inf, F32)
        l_scr[...] = jnp.zeros(l_scr.shape, F32)
        acc_scr[...] = jnp.zeros(acc_scr.shape, F32)

    def update(scores, pv_fn):
        s = jnp.concatenate(scores, axis=0)
        m_old = m_scr[...]
        m_new = jnp.maximum(m_old, jnp.max(s, axis=-1, keepdims=True))
        alpha = jnp.exp2(m_old - m_new)
        p = jnp.exp2(s - m_new)
        l_new = alpha * l_scr[...] + jnp.sum(p, axis=-1, keepdims=True)
        pv = jnp.concatenate([pv_fn(h, p[h * nrow:(h + 1) * nrow, :]) for h in range(N_HEADS)], axis=0)
        acc_new = alpha * acc_scr[...] + pv
        m_scr[...] = m_new
        l_scr[...] = l_new
        acc_scr[...] = acc_new
        return l_new, acc_new

    def head_keys(h):
        kt = [kp[h * hd:(h + 1) * hd, :].astype(BF16) for kp in k_pages]
        return kt[0] if len(kt) == 1 else jnp.concatenate(kt, axis=1)

    def head_values(h):
        vv = [vp[pl.ds(h, page_size, stride=N_HEADS), :].astype(BF16) for vp in v_pages]
        return vv[0] if len(vv) == 1 else jnp.concatenate(vv, axis=0)

    q_all = q_scr[...].astype(BF16)
    update([jnp.dot(q_all[h * nrow:(h + 1) * nrow, :], head_keys(h), preferred_element_type=F32)
            for h in range(N_HEADS)],
           lambda h, p: jnp.dot(p.astype(BF16), head_values(h), preferred_element_type=F32))

    @pl.when(step == pl.num_programs(1) - 1)
    def _():
        key = lax.broadcasted_iota(jnp.int32, (nrow, n_new), 1)
        qry = lax.broadcasted_iota(jnp.int32, (nrow, n_new), 0) % n_new
        q_f32 = q_scr[...]
        scores = []
        for h in range(N_HEADS):
            s = lax.dot_general(q_f32[h * nrow:(h + 1) * nrow, :], kn_ref[0, 0, :, h * hd:(h + 1) * hd],
                                (((1,), (1,)), ((), ())), preferred_element_type=F32)
            scores.append(jnp.where(key <= qry, s, -jnp.inf))
        l_fin, acc_fin = update(
            scores, lambda h, p: jnp.dot(p, vn_ref[0, 0, :, h * hd:(h + 1) * hd], preferred_element_type=F32))
        on = acc_fin / l_fin
        lam = _lambda_value(lam_ref, lam_init)
        wsub = wsub_ref[...]
        for h in range(N_HEADS):
            o = on[h * nrow:h * nrow + n_new, :] - lam * on[h * nrow + n_new:(h + 1) * nrow, :]
            o_ref[0, :, h * hd:(h + 1) * hd] = _rms(o, wsub) * (1.0 - lam_init)


def _sample_attn(proj, cache_k, cache_v, page_table, lam_params, w_subln, n_seq, n_new, lam_init):
    n_pages = page_table.shape[1]
    n_pool, page_size = cache_k.shape[0], cache_k.shape[1]
    pages_per_step = math.gcd(n_pages, 8)
    assert n_new % 8 == 0
    ckt = jnp.transpose(cache_k, (0, 2, 3, 4, 1)).reshape(n_pool, D_MODEL, page_size)
    cvr = cache_v.reshape(n_pool, page_size * N_HEADS, HEAD_DIM)
    proj4 = proj.reshape(N_GROUPS, n_seq, n_new, D_MODEL)

    def seq_blk(g):
        return pl.BlockSpec((1, 1, n_new, D_MODEL), lambda n, s, pt: (g, n, 0, 0))

    def k_spec(i):
        return pl.BlockSpec((None, D_MODEL, page_size),
                            lambda n, s, pt: (pt[n, s * pages_per_step + i], 0, 0))

    def v_spec(i):
        return pl.BlockSpec((None, page_size * N_HEADS, HEAD_DIM),
                            lambda n, s, pt: (pt[n, s * pages_per_step + i], 0, 0))

    nrows = N_HEADS * 2 * n_new
    grid_spec = pltpu.PrefetchScalarGridSpec(
        num_scalar_prefetch=1,
        grid=(n_seq, n_pages // pages_per_step),
        in_specs=[seq_blk(COL_QB), seq_blk(COL_KB), seq_blk(COL_VB),
                  pl.BlockSpec((4, QK_DIM), lambda n, s, pt: (0, 0)),
                  pl.BlockSpec((1, HEAD_DIM), lambda n, s, pt: (0, 0))]
                 + [k_spec(i) for i in range(pages_per_step)]
                 + [v_spec(i) for i in range(pages_per_step)],
        out_specs=pl.BlockSpec((1, n_new, D_MODEL), lambda n, s, pt: (n, 0, 0)),
        scratch_shapes=[pltpu.VMEM((nrows, HEAD_DIM), F32), pltpu.VMEM((nrows, 1), F32),
                        pltpu.VMEM((nrows, 1), F32), pltpu.VMEM((nrows, HEAD_DIM), F32)],
    )
    o = pl.pallas_call(
        functools.partial(_sample_attn_kernel, pages_per_step=pages_per_step, page_size=page_size,
                          n_new=n_new, lam_init=lam_init),
        grid_spec=grid_spec,
        out_shape=jax.ShapeDtypeStruct((n_seq, n_new, D_MODEL), F32),
        compiler_params=_compiler_params(("arbitrary", "arbitrary")),
        name="diff_attn_sample",
    )(page_table, proj4, proj4, proj4, lam_params, w_subln,
      *([ckt] * pages_per_step), *([cvr] * pages_per_step))
    return o.reshape(n_seq * n_new, D_MODEL)


def _merge_kernel(oa_ref, ob_ref, ga_ref, gb_ref, x_ref, wa_ref, wb_ref, wo_ref, g_ref, h_ref):
    ya = jnp.dot(oa_ref[...].astype(BF16), wa_ref[...], preferred_element_type=F32)
    yb = jnp.dot(ob_ref[...].astype(BF16), wb_ref[...], preferred_element_type=F32)
    y = ga_ref[0] * ya + gb_ref[0] * yb
    mix = jnp.dot(y.astype(BF16), wo_ref[...], preferred_element_type=F32)
    h_ref[...] = x_ref[...] + _rms(mix, g_ref[...])


def _merge(oa, ob, proj, x, wa, wb, wo, g_post, tm):
    n = x.shape[0]
    row = pl.BlockSpec((tm, D_MODEL), lambda i: (i, 0))
    wspec = pl.BlockSpec((D_MODEL, D_MODEL), lambda i: (0, 0))
    return pl.pallas_call(
        _merge_kernel,
        grid=(n // tm,),
        in_specs=[row, row,
                  pl.BlockSpec((1, tm, D_MODEL), lambda i: (COL_GA, i, 0)),
                  pl.BlockSpec((1, tm, D_MODEL), lambda i: (COL_GB, i, 0)),
                  row, wspec, wspec, wspec, pl.BlockSpec((1, D_MODEL), lambda i: (0, 0))],
        out_specs=row,
        out_shape=jax.ShapeDtypeStruct((n, D_MODEL), F32),
        compiler_params=_compiler_params(("arbitrary",)),
        name="merge_out",
    )(oa, ob, proj, proj, x, wa, wb, wo, g_post)


def _mem_attn_kernel(h_ref, mk_ref, mv_ref, gpre_ref, wq_ref, wo_ref, gpost_ref, o_ref, *, per_head_layout):
    h = h_ref[...]
    q = jnp.dot(_rms(h, gpre_ref[...]).astype(BF16), wq_ref[...], preferred_element_type=F32)
    q = (q * (MEM_DIM ** -0.5)).astype(BF16)
    outs = []
    for hh in range(MEM_HEADS):
        cols = slice(hh * MEM_DIM, (hh + 1) * MEM_DIM)
        if per_head_layout:
            mk, mv = mk_ref[0, :, hh, :], mv_ref[0, :, hh, :]
        else:
            mk, mv = mk_ref[0, :, cols], mv_ref[0, :, cols]
        s = lax.dot_general(q[:, cols], mk.astype(BF16), (((1,), (1,)), ((), ())),
                            preferred_element_type=F32)
        p = jnp.exp(s - jnp.max(s, axis=-1, keepdims=True))
        p = p / jnp.sum(p, axis=-1, keepdims=True)
        outs.append(jnp.dot(p.astype(BF16), mv.astype(BF16), preferred_element_type=F32))
    o = jnp.concatenate(outs, axis=-1).astype(BF16)
    mo = jnp.dot(o, wo_ref[...], preferred_element_type=F32)
    o_ref[...] = h + _rms(mo, gpost_ref[...])


def _mem_attn(h, mk, mv, g_pre, wq, wo, g_post, rows_per_batch, tm):
    n = h.shape[0]
    tiles_per_batch = rows_per_batch // tm
    per_head_layout = mk.ndim == 4
    row = pl.BlockSpec((tm, D_MODEL), lambda i: (i, 0))
    if per_head_layout:
        mem = pl.BlockSpec((1,) + mk.shape[1:], lambda i: (i // tiles_per_batch, 0, 0, 0))
    else:
        mem = pl.BlockSpec((1,) + mk.shape[1:], lambda i: (i // tiles_per_batch, 0, 0))
    wspec = pl.BlockSpec((D_MODEL, D_MODEL), lambda i: (0, 0))
    gspec = pl.BlockSpec((1, D_MODEL), lambda i: (0, 0))
    return pl.pallas_call(
        functools.partial(_mem_attn_kernel, per_head_layout=per_head_layout),
        grid=(n // tm,),
        in_specs=[row, mem, mem, gspec, wspec, wspec, gspec],
        out_specs=row,
        out_shape=jax.ShapeDtypeStruct((n, D_MODEL), F32),
        compiler_params=_compiler_params(("arbitrary",)),
        name="mem_attn",
    )(h, mk, mv, g_pre, wq, wo, g_post)


def _mlp_kernel(h_ref, gpre_ref, wup_ref, wdown_ref, gpost_ref, o_ref, xn_scr, acc_scr):
    j = pl.program_id(1)

    @pl.when(j == 0)
    def _():
        xn_scr[...] = _rms(h_ref[...], gpre_ref[...]).astype(BF16)
        acc_scr[...] = jnp.zeros_like(acc_scr)

    u = jnp.dot(xn_scr[...], wup_ref[...], preferred_element_type=F32)
    u = jnp.square(jnp.maximum(u, 0.0)).astype(BF16)
    acc_scr[...] += jnp.dot(u, wdown_ref[...], preferred_element_type=F32)

    @pl.when(j == pl.num_programs(1) - 1)
    def _():
        o_ref[...] = h_ref[...] + _rms(acc_scr[...], gpost_ref[...])


def _mlp(h, g_pre, wup, wdown, g_post, tm):
    n = h.shape[0]
    tf = D_MODEL
    row = pl.BlockSpec((tm, D_MODEL), lambda i, j: (i, 0))
    gspec = pl.BlockSpec((1, D_MODEL), lambda i, j: (0, 0))
    return pl.pallas_call(
        _mlp_kernel,
        grid=(n // tm, D_FF // tf),
        in_specs=[row, gspec,
                  pl.BlockSpec((D_MODEL, tf), lambda i, j: (0, j)),
                  pl.BlockSpec((tf, D_MODEL), lambda i, j: (j, 0)),
                  gspec],
        out_specs=row,
        out_shape=jax.ShapeDtypeStruct((n, D_MODEL), F32),
        scratch_shapes=[pltpu.VMEM((tm, D_MODEL), BF16), pltpu.VMEM((tm, D_MODEL), F32)],
        compiler_params=_compiler_params(("arbitrary", "arbitrary")),
        name="mlp",
    )(h, g_pre, wup, wdown, g_post)


def _layer(x, batch, seq, s0, past, mem_kv, p, lam_init):
    n = x.shape[0]
    tm = math.gcd(n, 512)
    proj, kt = _in_proj(x, p["w_pre_mix"], p["w_in"], batch, seq, transpose_k=past is None)
    o_a, s_new = _hgrn(proj, p["lb_logits"], p["w_hgrn_gnorm"], s0, batch, seq)
    if past is None:
        o_b = _prompt_attn(proj, kt, p["lam_params"], p["w_subln"], batch, seq, lam_init)
    else:
        o_b = _sample_attn(proj, past[0], past[1], past[2], p["lam_params"], p["w_subln"],
                           batch, seq, lam_init)
    h = _merge(o_a, o_b, proj, x, p["w_branch_a"], p["w_branch_b"], p["w_out"], p["w_post_mix"], tm)
    mk, mv = mem_kv
    h = _mem_attn(h, mk, mv, p["w_pre_mem"], p["w_mq"], p["w_mo"], p["w_post_mem"], seq,
                  math.gcd(seq, 512))
    h = _mlp(h, p["w_pre_mlp"], p["w_up"], p["w_down"], p["w_post_mlp"], tm)
    return h, s_new, proj, kt


def kernel(x_prompt, x_sample, cache_attn_k, cache_attn_v, state_hgrn, cache_mem_k, cache_mem_v,
           page_table, mem_prompt, w_pre_mix, w_in, hgrn_lb_logits, w_hgrn_gnorm,
           lambda_q1, lambda_k1, lambda_q2, lambda_k2, w_subln, w_branch_a, w_branch_b, w_out,
           w_post_mix, w_pre_mem, w_mem_norm, w_mq, w_mk, w_mv, w_mo, w_post_mem,
           w_pre_mlp, w_up, w_down, w_post_mlp):
    depth = w_in.shape[0]
    assert depth == 1, "single-layer step"
    l = 0
    batch, seq, _ = x_prompt.shape
    n_seq, n_new, _ = x_sample.shape
    mem_len = mem_prompt.shape[1]
    lam_init = 0.8 - 0.6 * math.exp(-0.3 * l)

    def gain(w):
        return w[l].reshape(1, -1)

    def weight(w):
        return w[l].astype(BF16)

    p = {
        "w_pre_mix": gain(w_pre_mix),
        "w_in": jnp.transpose(weight(w_in).reshape(D_MODEL, N_GROUPS, D_MODEL), (1, 0, 2)),
        "lb_logits": hgrn_lb_logits,
        "w_hgrn_gnorm": gain(w_hgrn_gnorm), "w_subln": gain(w_subln),
        "lam_params": jnp.stack([lambda_q1[l], lambda_k1[l], lambda_q2[l], lambda_k2[l]]),
        "w_branch_a": weight(w_branch_a), "w_branch_b": weight(w_branch_b), "w_out": weight(w_out),
        "w_post_mix": gain(w_post_mix), "w_pre_mem": gain(w_pre_mem), "w_mq": weight(w_mq),
        "w_mo": weight(w_mo), "w_post_mem": gain(w_post_mem), "w_pre_mlp": gain(w_pre_mlp),
        "w_up": weight(w_up), "w_down": weight(w_down), "w_post_mlp": gain(w_post_mlp),
    }

    memx = mem_prompt.reshape(batch * mem_len, D_MODEL)
    tm_mem = math.gcd(batch * mem_len, 512)
    mk_p = _norm_matmul(memx, gain(w_mem_norm), weight(w_mk), tm_mem).reshape(batch, mem_len, D_MODEL)
    mv_p = _norm_matmul(memx, gain(w_mem_norm), weight(w_mv), tm_mem).reshape(batch, mem_len, D_MODEL)
    hp, s_p, proj_p, kt_p = _layer(x_prompt.reshape(batch * seq, D_MODEL), batch, seq, None, None,
                                   (mk_p, mv_p), p, lam_init)
    k_p = jnp.transpose(kt_p.reshape(batch, N_HEADS, 2, QK_DIM, seq), (0, 4, 1, 2, 3))
    v_p = proj_p[COL_VB].reshape(batch, seq, N_HEADS, HEAD_DIM)

    past = (cache_attn_k[l], cache_attn_v[l], page_table)
    hs, s_s, proj_s, _ = _layer(x_sample.reshape(n_seq * n_new, D_MODEL), n_seq, n_new, state_hgrn[l],
                                past, (cache_mem_k[l], cache_mem_v[l]), p, lam_init)
    k_s = proj_s[COL_KB].reshape(n_seq, n_new, N_HEADS, 2, QK_DIM)
    v_s = proj_s[COL_VB].reshape(n_seq, n_new, N_HEADS, HEAD_DIM)

    return (hp.reshape(batch, seq, D_MODEL), hs.reshape(n_seq, n_new, D_MODEL),
            k_p[None], v_p[None], s_p[None],
            mk_p.reshape(1, batch, mem_len, MEM_HEADS, MEM_DIM),
            mv_p.reshape(1, batch, mem_len, MEM_HEADS, MEM_DIM),
            k_s[None], v_s[None], s_s[None])
```

```python
import functools
import math

import jax
import jax.numpy as jnp
from jax import lax
from jax.experimental import pallas as pl
from jax.experimental.pallas import tpu as pltpu

F32 = jnp.float32
BF16 = jnp.bfloat16

D_MODEL = 1024
N_HEADS = 8
HEAD_DIM = 128
QK_DIM = 64
MEM_HEADS = 4
MEM_DIM = 256
D_FF = 4 * D_MODEL
EPS = 1e-6
LOG2E = math.log2(math.e)
HGRN_CHUNK = 64
HGRN_SUB = 8
VMEM_LIMIT_BYTES = 56 * 1024 * 1024

COL_HQ, COL_F, COL_I, COL_G, COL_QB, COL_KB, COL_VB, COL_GA, COL_GB = range(9)
N_GROUPS = 9


def _sigmoid(x):
    return 1.0 / (1.0 + jnp.exp(-x))


def _rms(x, gain):
    ms = jnp.mean(x * x, axis=-1, keepdims=True)
    return x * lax.rsqrt(ms + EPS) * gain


def _compiler_params(semantics):
    return pltpu.CompilerParams(dimension_semantics=semantics, vmem_limit_bytes=VMEM_LIMIT_BYTES)


def _norm_matmul_kernel(x_ref, g_ref, w_ref, o_ref):
    xn = _rms(x_ref[...], g_ref[...]).astype(BF16)
    o_ref[...] = jnp.dot(xn, w_ref[...], preferred_element_type=F32)


def _norm_matmul(x, gain, w, tm):
    n, d = x.shape
    return pl.pallas_call(
        _norm_matmul_kernel,
        grid=(n // tm,),
        in_specs=[
            pl.BlockSpec((tm, d), lambda i: (i, 0)),
            pl.BlockSpec((1, d), lambda i: (0, 0)),
            pl.BlockSpec((d, D_MODEL), lambda i: (0, 0)),
        ],
        out_specs=pl.BlockSpec((tm, D_MODEL), lambda i: (i, 0)),
        out_shape=jax.ShapeDtypeStruct((n, D_MODEL), F32),
        compiler_params=_compiler_params(("arbitrary",)),
        name="norm_matmul",
    )(x, gain, w)


def _in_proj_kernel(x_ref, g_ref, w_ref, o_ref, *rest, transpose_k):
    if transpose_k:
        kt_ref, xn_scr = rest
    else:
        (xn_scr,) = rest
    j = pl.program_id(1)

    @pl.when(j == 0)
    def _():
        xn_scr[...] = _rms(x_ref[...], g_ref[...]).astype(BF16)

    y = jnp.dot(xn_scr[...], w_ref[j], preferred_element_type=F32)
    is_silu = (j == COL_HQ) | (j == COL_G)
    is_sigmoid = (j == COL_GA) | (j == COL_GB)
    sig = 0.5 * jnp.tanh(0.5 * y) + 0.5
    o_ref[0] = jnp.where(is_silu, y * sig, jnp.where(is_sigmoid, sig, y))

    if transpose_k:
        @pl.when(j == COL_KB)
        def _():
            kt_ref[0] = y.T


def _in_proj(x, gain, w_in, batch, seq, transpose_k):
    n = x.shape[0]
    tm = math.gcd(seq, 512) if transpose_k else math.gcd(n, 512)
    tiles_per_batch = seq // tm if transpose_k else 1
    out_specs = [pl.BlockSpec((1, tm, D_MODEL), lambda i, j: (j, i, 0))]
    out_shape = [jax.ShapeDtypeStruct((N_GROUPS, n, D_MODEL), F32)]
    if transpose_k:
        out_specs.append(pl.BlockSpec((1, D_MODEL, tm),
                                      lambda i, j: (i // tiles_per_batch, 0, i % tiles_per_batch)))
        out_shape.append(jax.ShapeDtypeStruct((batch, D_MODEL, seq), F32))
    res = pl.pallas_call(
        functools.partial(_in_proj_kernel, transpose_k=transpose_k),
        grid=(n // tm, N_GROUPS),
        in_specs=[
            pl.BlockSpec((tm, D_MODEL), lambda i, j: (i, 0)),
            pl.BlockSpec((1, D_MODEL), lambda i, j: (0, 0)),
            pl.BlockSpec((N_GROUPS, D_MODEL, D_MODEL), lambda i, j: (0, 0, 0),
                         pipeline_mode=pl.Buffered(1)),
        ],
        out_specs=out_specs,
        out_shape=out_shape,
        scratch_shapes=[pltpu.VMEM((tm, D_MODEL), BF16)],
        compiler_params=_compiler_params(("arbitrary", "arbitrary")),
        name="in_proj",
    )(x, gain, w_in)
    return (res[0], res[1]) if transpose_k else (res[0], None)


def _hgrn_levels(chunk):
    levels = []
    b = chunk // 2
    while b >= HGRN_SUB:
        levels.append(b)
        b //= 2
    return levels


def _hgrn_kernel(hq_ref, f_ref, v_ref, sg_ref, lbl_ref, wg_ref, s0_ref, o_ref, sfin_ref, st_scr,
                 *, chunk, n_chunks, has_s0):
    t_blk = pl.program_id(1)
    hd = HEAD_DIM

    @pl.when(t_blk == 0)
    def _():
        for h in range(N_HEADS):
            if has_s0:
                st_scr[h] = s0_ref[0, h].T
            else:
                st_scr[h] = jnp.zeros((hd, hd), F32)

    lbl = lbl_ref[...]
    lmax = jnp.max(lbl, axis=0, keepdims=True)
    lexp = jnp.exp(lbl - lmax)
    lb_all = lexp[0:1, :] / jnp.sum(lexp, axis=0, keepdims=True)

    levels = _hgrn_levels(chunk)
    row = lax.broadcasted_iota(jnp.int32, (chunk, hd), 0)
    tril = (lax.broadcasted_iota(jnp.int32, (chunk, chunk), 0)
            >= lax.broadcasted_iota(jnp.int32, (chunk, chunk), 1)).astype(F32)
    rr = lax.broadcasted_iota(jnp.int32, (chunk, chunk), 0)
    cc = lax.broadcasted_iota(jnp.int32, (chunk, chunk), 1)
    is_right = {b: (row & b) != 0 for b in levels}
    same_pair = {b: (rr // (2 * b)) == (cc // (2 * b)) for b in levels}
    sub_row = lax.broadcasted_iota(jnp.int32, (HGRN_SUB, hd), 0)
    wg = wg_ref[...]

    def chunk_body(c, carry):
        r0 = pl.multiple_of(c * chunk, chunk)
        rows = pl.ds(r0, chunk)
        for h in range(N_HEADS):
            cols = slice(h * hd, (h + 1) * hd)
            lb = lb_all[:, cols]
            q = hq_ref[0, 0, rows, cols]
            fa = f_ref[0, 0, rows, cols]
            v = v_ref[0, 0, rows, cols]
            f = lb + (1.0 - lb) * _sigmoid(fa)
            g = jnp.log(f)
            k = 1.0 - f
            cum = jnp.dot(tril, g, preferred_element_type=F32, precision=lax.Precision.HIGHEST)
            v_b = v.astype(BF16)

            o = jnp.zeros((chunk, hd), F32)
            if levels:
                a = jnp.zeros((chunk, chunk), F32)
                for b in levels:
                    pieces = []
                    for p in range(chunk // (2 * b)):
                        ref_row = cum[p * 2 * b + b - 1:p * 2 * b + b, :]
                        pieces.append(jnp.broadcast_to(ref_row, (2 * b, hd)))
                    ref = pieces[0] if len(pieces) == 1 else jnp.concatenate(pieces, axis=0)
                    e = jnp.exp(-jnp.abs(cum - ref))
                    ql = (q * jnp.where(is_right[b], e, 0.0)).astype(BF16)
                    kl = (k * jnp.where(is_right[b], 0.0, e)).astype(BF16)
                    al = lax.dot_general(ql, kl, (((1,), (1,)), ((), ())), preferred_element_type=F32)
                    a = a + jnp.where(same_pair[b], al, 0.0)
                o = jnp.dot(a.astype(BF16), v_b, preferred_element_type=F32)

            diag = []
            for j in range(chunk // HGRN_SUB):
                sl = slice(j * HGRN_SUB, (j + 1) * HGRN_SUB)
                qj, kj, cj, vj = q[sl], k[sl], cum[sl], v[sl]
                od = jnp.zeros((HGRN_SUB, hd), F32)
                for s in range(HGRN_SUB):
                    dec = jnp.exp(jnp.minimum(cj - cj[s:s + 1, :], 0.0))
                    dec = jnp.where(sub_row >= s, dec, 0.0)
                    a_s = jnp.sum(qj * dec * kj[s:s + 1, :], axis=-1, keepdims=True)
                    od = od + a_s * vj[s:s + 1, :]
                diag.append(od)
            o = o + (diag[0] if len(diag) == 1 else jnp.concatenate(diag, axis=0))

            st = st_scr[h]
            q0 = (q * jnp.exp(cum)).astype(BF16)
            o = o + lax.dot_general(q0, st.astype(BF16), (((1,), (1,)), ((), ())),
                                    preferred_element_type=F32)
            last = cum[chunk - 1:chunk, :]
            kd = (k * jnp.exp(last - cum)).astype(BF16)
            upd = jnp.dot(v.T.astype(BF16), kd, preferred_element_type=F32)
            st_scr[h] = st * jnp.exp(last) + upd

            o_ref[0, rows, cols] = _rms(o, wg) * sg_ref[0, 0, rows, cols]
        return carry

    lax.fori_loop(0, n_chunks, chunk_body, 0)

    @pl.when(t_blk == pl.num_programs(1) - 1)
    def _():
        for h in range(N_HEADS):
            sfin_ref[0, h] = st_scr[h].T


def _hgrn(proj, lb_logits, w_gnorm, s0, batch, seq):
    chunk = math.gcd(seq, HGRN_CHUNK)
    assert chunk % HGRN_SUB == 0
    tb = math.gcd(seq, 512)
    n_tb = seq // tb
    proj4 = proj.reshape(N_GROUPS, batch, seq, D_MODEL)
    has_s0 = s0 is not None
    if not has_s0:
        s0 = jnp.zeros((1, N_HEADS, HEAD_DIM, HEAD_DIM), F32)
    s0_map = (lambda b, t: (b, 0, 0, 0)) if has_s0 else (lambda b, t: (0, 0, 0, 0))
    n_lb = lb_logits.shape[0]

    def group(g):
        return pl.BlockSpec((1, 1, tb, D_MODEL), lambda b, t: (g, b, t, 0))

    o, s_fin = pl.pallas_call(
        functools.partial(_hgrn_kernel, chunk=chunk, n_chunks=tb // chunk, has_s0=has_s0),
        grid=(batch, n_tb),
        in_specs=[
            group(COL_HQ), group(COL_F), group(COL_I), group(COL_G),
            pl.BlockSpec((n_lb, D_MODEL), lambda b, t: (0, 0)),
            pl.BlockSpec((1, HEAD_DIM), lambda b, t: (0, 0)),
            pl.BlockSpec((1, N_HEADS, HEAD_DIM, HEAD_DIM), s0_map),
        ],
        out_specs=[
            pl.BlockSpec((1, tb, D_MODEL), lambda b, t: (b, t, 0)),
            pl.BlockSpec((1, N_HEADS, HEAD_DIM, HEAD_DIM), lambda b, t: (b, 0, 0, 0)),
        ],
        out_shape=[
            jax.ShapeDtypeStruct((batch, seq, D_MODEL), F32),
            jax.ShapeDtypeStruct((batch, N_HEADS, HEAD_DIM, HEAD_DIM), F32),
        ],
        scratch_shapes=[pltpu.VMEM((N_HEADS, HEAD_DIM, HEAD_DIM), F32)],
        compiler_params=_compiler_params(("arbitrary", "arbitrary")),
        name="hgrn2",
    )(proj4, proj4, proj4, proj4, lb_logits, w_gnorm, s0)
    return o.reshape(batch * seq, D_MODEL), s_fin


def _lambda_value(lam_ref, lam_init):
    l = lam_ref[...]
    a = jnp.sum(l[0:1, :] * l[1:2, :], axis=-1, keepdims=True)
    b = jnp.sum(l[2:3, :] * l[3:4, :], axis=-1, keepdims=True)
    return jnp.exp(a) - jnp.exp(b) + lam_init


def _split_maps(qh):
    lane = lax.broadcasted_iota(jnp.int32, qh.shape, 1)
    return jnp.concatenate([jnp.where(lane < QK_DIM, qh, 0.0), jnp.where(lane >= QK_DIM, qh, 0.0)], axis=0)


RING_SLOTS = 3
PAGES_PER_ITEM = 4


def _fused_attn_kernel(pt_ref, q_ref, kt_ref, v_ref, lam_ref, wsub_ref, qs_ref, kn_ref, vn_ref,
                       ck_hbm, cv_hbm, o_ref, os_ref,
                       ktb_scr, vb_scr, acc_scr, kbuf, vbuf, sem, sq_scr, sm_scr, sl_scr, sacc_scr,
                       *, tq, n_q, n_grid_steps, lam_init, n_new, page_size, groups_per_seq, n_items):
    hd = HEAD_DIM
    nrow = 2 * n_new
    scale = QK_DIM ** -0.5 * LOG2E
    step = pl.program_id(0) * pl.num_programs(1) + pl.program_id(1)
    kv_per_step = n_q * (n_q + 1) // 2
    lam = _lambda_value(lam_ref, lam_init)
    wsub = wsub_ref[...]

    def page_copies(item, lookup):
        slot = item % RING_SLOTS
        seq = item // groups_per_seq
        first_page = (item % groups_per_seq) * PAGES_PER_ITEM
        copies = []
        for i in range(PAGES_PER_ITEM):
            page = pt_ref[seq, first_page + i] if lookup else 0
            copies.append(pltpu.make_async_copy(ck_hbm.at[page], kbuf.at[slot, i], sem.at[slot]))
            copies.append(pltpu.make_async_copy(cv_hbm.at[page], vbuf.at[slot, i], sem.at[slot]))
        return copies

    def start_item(item):
        for c in page_copies(item, True):
            c.start()

    def wait_item(item):
        for c in page_copies(item, False):
            c.wait()

    def sample_update(scores, pv_fn, valid):
        s = jnp.concatenate(scores, axis=0)
        m_old, l_old, acc_old = sm_scr[...], sl_scr[...], sacc_scr[...]
        m_new = jnp.maximum(m_old, jnp.max(s, axis=-1, keepdims=True))
        alpha = jnp.exp2(m_old - m_new)
        p = jnp.exp2(s - m_new)
        l_new = alpha * l_old + jnp.sum(p, axis=-1, keepdims=True)
        pv = jnp.concatenate([pv_fn(h, p[h * nrow:(h + 1) * nrow, :]) for h in range(N_HEADS)], axis=0)
        acc_new = alpha * acc_old + pv
        if valid is not None:
            m_new = jnp.where(valid, m_new, m_old)
            l_new = jnp.where(valid, l_new, l_old)
            acc_new = jnp.where(valid, acc_new, acc_old)
        sm_scr[...] = m_new
        sl_scr[...] = l_new
        sacc_scr[...] = acc_new
        return l_new, acc_new

    def finish_sequence(seq):
        key = lax.broadcasted_iota(jnp.int32, (nrow, n_new), 1)
        qry = lax.broadcasted_iota(jnp.int32, (nrow, n_new), 0) % n_new
        q_f32 = sq_scr[...]
        kn = kn_ref[0, seq]
        vn = vn_ref[0, seq]
        scores = []
        for h in range(N_HEADS):
            s = lax.dot_general(q_f32[h * nrow:(h + 1) * nrow, :], kn[:, h * hd:(h + 1) * hd],
                                (((1,), (1,)), ((), ())), preferred_element_type=F32)
            scores.append(jnp.where(key <= qry, s, -jnp.inf))
        l_fin, acc_fin = sample_update(
            scores, lambda h, p: jnp.dot(p, vn[:, h * hd:(h + 1) * hd], preferred_element_type=F32), None)
        on = acc_fin / l_fin
        for h in range(N_HEADS):
            o = on[h * nrow:h * nrow + n_new, :] - lam * on[h * nrow + n_new:(h + 1) * nrow, :]
            os_ref[seq, :, h * hd:(h + 1) * hd] = _rms(o, wsub) * (1.0 - lam_init)

    def sample_item(item, valid):
        def when(c, fn):
            if valid is not None:
                c = valid if c is None else jnp.logical_and(valid, c)
            if c is None:
                fn()
            else:
                pl.when(c)(fn)

        seq = item // groups_per_seq
        grp = item % groups_per_seq
        slot = item % RING_SLOTS

        def begin_sequence():
            q = qs_ref[0, seq] * scale
            for h in range(N_HEADS):
                sq_scr[h * nrow:(h + 1) * nrow, :] = _split_maps(q[:, h * hd:(h + 1) * hd])
            sm_scr[...] = jnp.full(sm_scr.shape, -jnp.inf, F32)
            sl_scr[...] = jnp.zeros(sl_scr.shape, F32)
            sacc_scr[...] = jnp.zeros(sacc_scr.shape, F32)

        def prime_ring():
            for first in range(min(RING_SLOTS - 1, n_items)):
                start_item(first)

        when(grp == 0, begin_sequence)
        when(item == 0, prime_ring)
        pl.when(item + (RING_SLOTS - 1) < n_items)(lambda: start_item(item + (RING_SLOTS - 1)))
        when(None, lambda: wait_item(item))

        q_all = sq_scr[...].astype(BF16)

        def head_keys(h):
            return jnp.concatenate([kbuf[slot, i, h * hd:(h + 1) * hd, :].astype(BF16)
                                    for i in range(PAGES_PER_ITEM)], axis=1)

        def head_values(h):
            return jnp.concatenate([vbuf.at[slot, i][pl.ds(h, page_size, stride=N_HEADS), :].astype(BF16)
                                    for i in range(PAGES_PER_ITEM)], axis=0)

        sample_update([jnp.dot(q_all[h * nrow:(h + 1) * nrow, :], head_keys(h), preferred_element_type=F32)
                       for h in range(N_HEADS)],
                      lambda h, p: jnp.dot(p.astype(BF16), head_values(h), preferred_element_type=F32),
                      valid)
        when(grp == groups_per_seq - 1, lambda: finish_sequence(seq))

    ktb_scr[...] = kt_ref[0].astype(BF16)
    vb_scr[...] = v_ref[0, 0].astype(BF16)
    tri = (lax.broadcasted_iota(jnp.int32, (2 * tq, tq), 1)
           <= (lax.broadcasted_iota(jnp.int32, (2 * tq, tq), 0) % tq))

    def q_body(qi, carry):
        q0 = pl.multiple_of(qi * tq, tq)
        qq = _split_maps(q_ref[0, 0, pl.ds(q0, tq), :] * scale).astype(BF16)
        acc_scr[...] = jnp.zeros((2 * tq, HEAD_DIM), F32)
        first_item = step * kv_per_step + (qi * (qi + 1)) // 2

        def kv_step(j, ml, masked):
            item = first_item + j
            sample_item(item, item < n_items)
            m, l = ml
            k0 = pl.multiple_of(j * tq, tq)
            s = jnp.dot(qq, ktb_scr[:, pl.ds(k0, tq)], preferred_element_type=F32)
            if masked:
                s = jnp.where(tri, s, -jnp.inf)
            m_new = jnp.maximum(m, jnp.max(s, axis=-1, keepdims=True))
            alpha = jnp.exp2(m - m_new)
            p = jnp.exp2(s - m_new)
            l_new = alpha * l + jnp.sum(p, axis=-1, keepdims=True)
            acc_scr[...] = alpha * acc_scr[...] + jnp.dot(p.astype(BF16), vb_scr[pl.ds(k0, tq), :],
                                                          preferred_element_type=F32)
            return m_new, l_new

        init = (jnp.full((2 * tq, 1), -jnp.inf, F32), jnp.zeros((2 * tq, 1), F32))
        ml = lax.fori_loop(0, qi, lambda j, ml: kv_step(j, ml, False), init)
        m, l = kv_step(qi, ml, True)
        on = acc_scr[...] / l
        o = on[0:tq, :] - lam * on[tq:2 * tq, :]
        o_ref[0, pl.ds(q0, tq), :] = _rms(o, wsub) * (1.0 - lam_init)
        return carry

    lax.fori_loop(0, n_q, q_body, 0)

    if n_grid_steps * kv_per_step < n_items:
        @pl.when(step == n_grid_steps - 1)
        def _():
            def tail(item, carry):
                sample_item(item, None)
                return carry
            lax.fori_loop(n_grid_steps * kv_per_step, n_items, tail, 0)


def _fused_attn(proj_p, kt, proj_s, cache_k, cache_v, page_table, lam_params, w_subln,
                batch, seq, n_seq, n_new, lam_init):
    tq = math.gcd(seq, 512)
    n_q = seq // tq
    n_pages = page_table.shape[1]
    n_pool, page_size = cache_k.shape[0], cache_k.shape[1]
    assert n_new % 8 == 0 and n_pages % PAGES_PER_ITEM == 0
    groups_per_seq = n_pages // PAGES_PER_ITEM
    ckt = jnp.transpose(cache_k, (0, 2, 3, 4, 1)).reshape(n_pool, D_MODEL, page_size)
    cvr = cache_v.reshape(n_pool, page_size * N_HEADS, HEAD_DIM)
    proj4_p = proj_p.reshape(N_GROUPS, batch, seq, D_MODEL)
    proj4_s = proj_s.reshape(N_GROUPS, n_seq, n_new, D_MODEL)
    nrows = N_HEADS * 2 * n_new

    def sample_group(g):
        return pl.BlockSpec((1, n_seq, n_new, D_MODEL), lambda b, h, pt: (g, 0, 0, 0))

    grid_spec = pltpu.PrefetchScalarGridSpec(
        num_scalar_prefetch=1,
        grid=(batch, N_HEADS),
        in_specs=[pl.BlockSpec((1, 1, seq, HEAD_DIM), lambda b, h, pt: (COL_QB, b, 0, h)),
                  pl.BlockSpec((1, HEAD_DIM, seq), lambda b, h, pt: (b, h, 0)),
                  pl.BlockSpec((1, 1, seq, HEAD_DIM), lambda b, h, pt: (COL_VB, b, 0, h)),
                  pl.BlockSpec((4, QK_DIM), lambda b, h, pt: (0, 0)),
                  pl.BlockSpec((1, HEAD_DIM), lambda b, h, pt: (0, 0)),
                  sample_group(COL_QB), sample_group(COL_KB), sample_group(COL_VB),
                  pl.BlockSpec(memory_space=pl.ANY), pl.BlockSpec(memory_space=pl.ANY)],
        out_specs=[pl.BlockSpec((1, seq, HEAD_DIM), lambda b, h, pt: (b, 0, h)),
                   pl.BlockSpec((n_seq, n_new, D_MODEL), lambda b, h, pt: (0, 0, 0))],
        scratch_shapes=[pltpu.VMEM((HEAD_DIM, seq), BF16), pltpu.VMEM((seq, HEAD_DIM), BF16),
                        pltpu.VMEM((2 * tq, HEAD_DIM), F32),
                        pltpu.VMEM((RING_SLOTS, PAGES_PER_ITEM, D_MODEL, page_size), F32),
                        pltpu.VMEM((RING_SLOTS, PAGES_PER_ITEM, page_size * N_HEADS, HEAD_DIM), F32),
                        pltpu.SemaphoreType.DMA((RING_SLOTS,)),
                        pltpu.VMEM((nrows, HEAD_DIM), F32), pltpu.VMEM((nrows, 1), F32),
                        pltpu.VMEM((nrows, 1), F32), pltpu.VMEM((nrows, HEAD_DIM), F32)],
    )
    o_p, o_s = pl.pallas_call(
        functools.partial(_fused_attn_kernel, tq=tq, n_q=n_q, n_grid_steps=batch * N_HEADS,
                          lam_init=lam_init, n_new=n_new, page_size=page_size,
                          groups_per_seq=groups_per_seq, n_items=n_seq * groups_per_seq),
        grid_spec=grid_spec,
        out_shape=[jax.ShapeDtypeStruct((batch, seq, D_MODEL), F32),
                   jax.ShapeDtypeStruct((n_seq, n_new, D_MODEL), F32)],
        compiler_params=_compiler_params(("arbitrary", "arbitrary")),
        name="diff_attn_fused",
    )(page_table, proj4_p, kt, proj4_p, lam_params, w_subln, proj4_s, proj4_s, proj4_s, ckt, cvr)
    return o_p.reshape(batch * seq, D_MODEL), o_s.reshape(n_seq * n_new, D_MODEL)


def _merge_kernel(oa_ref, ob_ref, ga_ref, gb_ref, x_ref, wa_ref, wb_ref, wo_ref, g_ref, h_ref):
    ya = jnp.dot(oa_ref[...].astype(BF16), wa_ref[...], preferred_element_type=F32)
    yb = jnp.dot(ob_ref[...].astype(BF16), wb_ref[...], preferred_element_type=F32)
    y = ga_ref[0] * ya + gb_ref[0] * yb
    mix = jnp.dot(y.astype(BF16), wo_ref[...], preferred_element_type=F32)
    h_ref[...] = x_ref[...] + _rms(mix, g_ref[...])


def _merge(oa, ob, proj, x, wa, wb, wo, g_post, tm):
    n = x.shape[0]
    row = pl.BlockSpec((tm, D_MODEL), lambda i: (i, 0))
    wspec = pl.BlockSpec((D_MODEL, D_MODEL), lambda i: (0, 0))
    return pl.pallas_call(
        _merge_kernel,
        grid=(n // tm,),
        in_specs=[row, row,
                  pl.BlockSpec((1, tm, D_MODEL), lambda i: (COL_GA, i, 0)),
                  pl.BlockSpec((1, tm, D_MODEL), lambda i: (COL_GB, i, 0)),
                  row, wspec, wspec, wspec, pl.BlockSpec((1, D_MODEL), lambda i: (0, 0))],
        out_specs=row,
        out_shape=jax.ShapeDtypeStruct((n, D_MODEL), F32),
        compiler_params=_compiler_params(("arbitrary",)),
        name="merge_out",
    )(oa, ob, proj, proj, x, wa, wb, wo, g_post)


def _mem_attn_kernel(h_ref, mk_ref, mv_ref, gpre_ref, wq_ref, wo_ref, gpost_ref, o_ref, *, per_head_layout):
    h = h_ref[...]
    q = jnp.dot(_rms(h, gpre_ref[...]).astype(BF16), wq_ref[...], preferred_element_type=F32)
    q = (q * (MEM_DIM ** -0.5)).astype(BF16)
    outs = []
    for hh in range(MEM_HEADS):
        cols = slice(hh * MEM_DIM, (hh + 1) * MEM_DIM)
        if per_head_layout:
            mk, mv = mk_ref[0, :, hh, :], mv_ref[0, :, hh, :]
        else:
            mk, mv = mk_ref[0, :, cols], mv_ref[0, :, cols]
        s = lax.dot_general(q[:, cols], mk.astype(BF16), (((1,), (1,)), ((), ())),
                            preferred_element_type=F32)
        p = jnp.exp(s - jnp.max(s, axis=-1, keepdims=True))
        p = p / jnp.sum(p, axis=-1, keepdims=True)
        outs.append(jnp.dot(p.astype(BF16), mv.astype(BF16), preferred_element_type=F32))
    o = jnp.concatenate(outs, axis=-1).astype(BF16)
    mo = jnp.dot(o, wo_ref[...], preferred_element_type=F32)
    o_ref[...] = h + _rms(mo, gpost_ref[...])


def _mem_attn(h, mk, mv, g_pre, wq, wo, g_post, rows_per_batch, tm):
    n = h.shape[0]
    tiles_per_batch = rows_per_batch // tm
    per_head_layout = mk.ndim == 4
    row = pl.BlockSpec((tm, D_MODEL), lambda i: (i, 0))
    if per_head_layout:
        mem = pl.BlockSpec((1,) + mk.shape[1:], lambda i: (i // tiles_per_batch, 0, 0, 0))
    else:
        mem = pl.BlockSpec((1,) + mk.shape[1:], lambda i: (i // tiles_per_batch, 0, 0))
    wspec = pl.BlockSpec((D_MODEL, D_MODEL), lambda i: (0, 0))
    gspec = pl.BlockSpec((1, D_MODEL), lambda i: (0, 0))
    return pl.pallas_call(
        functools.partial(_mem_attn_kernel, per_head_layout=per_head_layout),
        grid=(n // tm,),
        in_specs=[row, mem, mem, gspec, wspec, wspec, gspec],
        out_specs=row,
        out_shape=jax.ShapeDtypeStruct((n, D_MODEL), F32),
        compiler_params=_compiler_params(("arbitrary",)),
        name="mem_attn",
    )(h, mk, mv, g_pre, wq, wo, g_post)


def _mlp_kernel(h_ref, gpre_ref, wup_ref, wdown_ref, gpost_ref, o_ref, xn_scr, acc_scr):
    j = pl.program_id(1)

    @pl.when(j == 0)
    def _():
        xn_scr[...] = _rms(h_ref[...], gpre_ref[...]).astype(BF16)
        acc_scr[...] = jnp.zeros_like(acc_scr)

    u = jnp.dot(xn_scr[...], wup_ref[...], preferred_element_type=F32)
    u = jnp.square(jnp.maximum(u, 0.0)).astype(BF16)
    acc_scr[...] += jnp.dot(u, wdown_ref[...], preferred_element_type=F32)

    @pl.when(j == pl.num_programs(1) - 1)
    def _():
        o_ref[...] = h_ref[...] + _rms(acc_scr[...], gpost_ref[...])


def _mlp(h, g_pre, wup, wdown, g_post, tm):
    n = h.shape[0]
    tf = D_MODEL
    row = pl.BlockSpec((tm, D_MODEL), lambda i, j: (i, 0))
    gspec = pl.BlockSpec((1, D_MODEL), lambda i, j: (0, 0))
    return pl.pallas_call(
        _mlp_kernel,
        grid=(n // tm, D_FF // tf),
        in_specs=[row, gspec,
                  pl.BlockSpec((D_MODEL, tf), lambda i, j: (0, j)),
                  pl.BlockSpec((tf, D_MODEL), lambda i, j: (j, 0)),
                  gspec],
        out_specs=row,
        out_shape=jax.ShapeDtypeStruct((n, D_MODEL), F32),
        scratch_shapes=[pltpu.VMEM((tm, D_MODEL), BF16), pltpu.VMEM((tm, D_MODEL), F32)],
        compiler_params=_compiler_params(("arbitrary", "arbitrary")),
        name="mlp",
    )(h, g_pre, wup, wdown, g_post)


def _post_attn(o_a, o_b, proj, x, seq, mem_kv, p):
    tm = math.gcd(x.shape[0], 512)
    h = _merge(o_a, o_b, proj, x, p["w_branch_a"], p["w_branch_b"], p["w_out"], p["w_post_mix"], tm)
    mk, mv = mem_kv
    h = _mem_attn(h, mk, mv, p["w_pre_mem"], p["w_mq"], p["w_mo"], p["w_post_mem"], seq,
                  math.gcd(seq, 512))
    return _mlp(h, p["w_pre_mlp"], p["w_up"], p["w_down"], p["w_post_mlp"], tm)


def kernel(x_prompt, x_sample, cache_attn_k, cache_attn_v, state_hgrn, cache_mem_k, cache_mem_v,
           page_table, mem_prompt, w_pre_mix, w_in, hgrn_lb_logits, w_hgrn_gnorm,
           lambda_q1, lambda_k1, lambda_q2, lambda_k2, w_subln, w_branch_a, w_branch_b, w_out,
           w_post_mix, w_pre_mem, w_mem_norm, w_mq, w_mk, w_mv, w_mo, w_post_mem,
           w_pre_mlp, w_up, w_down, w_post_mlp):
    depth = w_in.shape[0]
    assert depth == 1, "single-layer step"
    l = 0
    batch, seq, _ = x_prompt.shape
    n_seq, n_new, _ = x_sample.shape
    mem_len = mem_prompt.shape[1]
    lam_init = 0.8 - 0.6 * math.exp(-0.3 * l)

    def gain(w):
        return w[l].reshape(1, -1)

    def weight(w):
        return w[l].astype(BF16)

    p = {
        "w_pre_mix": gain(w_pre_mix),
        "w_in": jnp.transpose(weight(w_in).reshape(D_MODEL, N_GROUPS, D_MODEL), (1, 0, 2)),
        "lb_logits": hgrn_lb_logits,
        "w_hgrn_gnorm": gain(w_hgrn_gnorm), "w_subln": gain(w_subln),
        "lam_params": jnp.stack([lambda_q1[l], lambda_k1[l], lambda_q2[l], lambda_k2[l]]),
        "w_branch_a": weight(w_branch_a), "w_branch_b": weight(w_branch_b), "w_out": weight(w_out),
        "w_post_mix": gain(w_post_mix), "w_pre_mem": gain(w_pre_mem), "w_mq": weight(w_mq),
        "w_mo": weight(w_mo), "w_post_mem": gain(w_post_mem), "w_pre_mlp": gain(w_pre_mlp),
        "w_up": weight(w_up), "w_down": weight(w_down), "w_post_mlp": gain(w_post_mlp),
    }

    memx = mem_prompt.reshape(batch * mem_len, D_MODEL)
    tm_mem = math.gcd(batch * mem_len, 512)
    mk_p = _norm_matmul(memx, gain(w_mem_norm), weight(w_mk), tm_mem).reshape(batch, mem_len, D_MODEL)
    mv_p = _norm_matmul(memx, gain(w_mem_norm), weight(w_mv), tm_mem).reshape(batch, mem_len, D_MODEL)
    xp = x_prompt.reshape(batch * seq, D_MODEL)
    proj_p, kt_p = _in_proj(xp, p["w_pre_mix"], p["w_in"], batch, seq, transpose_k=True)
    oa_p, s_p = _hgrn(proj_p, p["lb_logits"], p["w_hgrn_gnorm"], None, batch, seq)

    xs = x_sample.reshape(n_seq * n_new, D_MODEL)
    proj_s, _ = _in_proj(xs, p["w_pre_mix"], p["w_in"], n_seq, n_new, transpose_k=False)
    oa_s, s_s = _hgrn(proj_s, p["lb_logits"], p["w_hgrn_gnorm"], state_hgrn[l], n_seq, n_new)

    ob_p, ob_s = _fused_attn(proj_p, kt_p, proj_s, cache_attn_k[l], cache_attn_v[l], page_table,
                             p["lam_params"], p["w_subln"], batch, seq, n_seq, n_new, lam_init)
    hp = _post_attn(oa_p, ob_p, proj_p, xp, seq, (mk_p, mv_p), p)
    hs = _post_attn(oa_s, ob_s, proj_s, xs, n_new, (cache_mem_k[l], cache_mem_v[l]), p)
    k_p = jnp.transpose(kt_p.reshape(batch, N_HEADS, 2, QK_DIM, seq), (0, 4, 1, 2, 3))
    v_p = proj_p[COL_VB].reshape(batch, seq, N_HEADS, HEAD_DIM)
    k_s = proj_s[COL_KB].reshape(n_seq, n_new, N_HEADS, 2, QK_DIM)
    v_s = proj_s[COL_VB].reshape(n_seq, n_new, N_HEADS, HEAD_DIM)

    return (hp.reshape(batch, seq, D_MODEL), hs.reshape(n_seq, n_new, D_MODEL),
            k_p[None], v_p[None], s_p[None],
            mk_p.reshape(1, batch, mem_len, MEM_HEADS, MEM_DIM),
            mv_p.reshape(1, batch, mem_len, MEM_HEADS, MEM_DIM),
            k_s[None], v_s[None], s_s[None])
```

```python
import functools
import math

import jax
import jax.numpy as jnp
from jax import lax
from jax.experimental import pallas as pl
from jax.experimental.pallas import tpu as pltpu

F32 = jnp.float32
BF16 = jnp.bfloat16

D_MODEL = 1024
N_HEADS = 8
HEAD_DIM = 128
QK_DIM = 64
MEM_HEADS = 4
MEM_DIM = 256
D_FF = 4 * D_MODEL
EPS = 1e-6
LOG2E = math.log2(math.e)
HGRN_CHUNK = 64
HGRN_SUB = 8
VMEM_LIMIT_BYTES = 56 * 1024 * 1024

COL_HQ, COL_F, COL_I, COL_G, COL_QB, COL_KB, COL_VB, COL_GA, COL_GB = range(9)
N_GROUPS = 9


def _sigmoid(x):
    return 1.0 / (1.0 + jnp.exp(-x))


def _rms(x, gain):
    ms = jnp.mean(x * x, axis=-1, keepdims=True)
    return x * lax.rsqrt(ms + EPS) * gain


def _compiler_params(semantics):
    return pltpu.CompilerParams(dimension_semantics=semantics, vmem_limit_bytes=VMEM_LIMIT_BYTES)


def _norm_matmul_kernel(x_ref, g_ref, w_ref, o_ref):
    xn = _rms(x_ref[...], g_ref[...]).astype(BF16)
    o_ref[...] = jnp.dot(xn, w_ref[...], preferred_element_type=F32)


def _norm_matmul(x, gain, w, tm):
    n, d = x.shape
    return pl.pallas_call(
        _norm_matmul_kernel,
        grid=(n // tm,),
        in_specs=[
            pl.BlockSpec((tm, d), lambda i: (i, 0)),
            pl.BlockSpec((1, d), lambda i: (0, 0)),
            pl.BlockSpec((d, D_MODEL), lambda i: (0, 0)),
        ],
        out_specs=pl.BlockSpec((tm, D_MODEL), lambda i: (i, 0)),
        out_shape=jax.ShapeDtypeStruct((n, D_MODEL), F32),
        compiler_params=_compiler_params(("arbitrary",)),
        name="norm_matmul",
    )(x, gain, w)


def _in_proj_kernel(x_ref, g_ref, w_ref, o_ref, *rest, transpose_k):
    if transpose_k:
        kt_ref, xn_scr = rest
    else:
        (xn_scr,) = rest
    j = pl.program_id(1)

    @pl.when(j == 0)
    def _():
        xn_scr[...] = _rms(x_ref[...], g_ref[...]).astype(BF16)

    y = jnp.dot(xn_scr[...], w_ref[j], preferred_element_type=F32)
    is_silu = (j == COL_HQ) | (j == COL_G)
    is_sigmoid = (j == COL_GA) | (j == COL_GB)
    sig = 0.5 * jnp.tanh(0.5 * y) + 0.5
    o_ref[0] = jnp.where(is_silu, y * sig, jnp.where(is_sigmoid, sig, y))

    if transpose_k:
        @pl.when(j == COL_KB)
        def _():
            kt_ref[0] = y.T


def _in_proj(x, gain, w_in, batch, seq, transpose_k):
    n = x.shape[0]
    tm = math.gcd(seq, 512) if transpose_k else math.gcd(n, 512)
    tiles_per_batch = seq // tm if transpose_k else 1
    out_specs = [pl.BlockSpec((1, tm, D_MODEL), lambda i, j: (j, i, 0))]
    out_shape = [jax.ShapeDtypeStruct((N_GROUPS, n, D_MODEL), F32)]
    if transpose_k:
        out_specs.append(pl.BlockSpec((1, D_MODEL, tm),
                                      lambda i, j: (i // tiles_per_batch, 0, i % tiles_per_batch)))
        out_shape.append(jax.ShapeDtypeStruct((batch, D_MODEL, seq), F32))
    res = pl.pallas_call(
        functools.partial(_in_proj_kernel, transpose_k=transpose_k),
        grid=(n // tm, N_GROUPS),
        in_specs=[
            pl.BlockSpec((tm, D_MODEL), lambda i, j: (i, 0)),
            pl.BlockSpec((1, D_MODEL), lambda i, j: (0, 0)),
            pl.BlockSpec((N_GROUPS, D_MODEL, D_MODEL), lambda i, j: (0, 0, 0),
                         pipeline_mode=pl.Buffered(1)),
        ],
        out_specs=out_specs,
        out_shape=out_shape,
        scratch_shapes=[pltpu.VMEM((tm, D_MODEL), BF16)],
        compiler_params=_compiler_params(("arbitrary", "arbitrary")),
        name="in_proj",
    )(x, gain, w_in)
    return (res[0], res[1]) if transpose_k else (res[0], None)


def _hgrn_levels(chunk):
    levels = []
    b = chunk // 2
    while b >= HGRN_SUB:
        levels.append(b)
        b //= 2
    return levels


def _hgrn_kernel(hq_ref, f_ref, v_ref, sg_ref, lbl_ref, wg_ref, s0_ref, o_ref, sfin_ref, st_scr,
                 *, chunk, n_chunks, has_s0):
    t_blk = pl.program_id(1)
    hd = HEAD_DIM

    @pl.when(t_blk == 0)
    def _():
        for h in range(N_HEADS):
            if has_s0:
                st_scr[h] = s0_ref[0, h].T
            else:
                st_scr[h] = jnp.zeros((hd, hd), F32)

    lbl = lbl_ref[...]
    lmax = jnp.max(lbl, axis=0, keepdims=True)
    lexp = jnp.exp(lbl - lmax)
    lb_all = lexp[0:1, :] / jnp.sum(lexp, axis=0, keepdims=True)

    levels = _hgrn_levels(chunk)
    row = lax.broadcasted_iota(jnp.int32, (chunk, hd), 0)
    tril = (lax.broadcasted_iota(jnp.int32, (chunk, chunk), 0)
            >= lax.broadcasted_iota(jnp.int32, (chunk, chunk), 1)).astype(F32)
    rr = lax.broadcasted_iota(jnp.int32, (chunk, chunk), 0)
    cc = lax.broadcasted_iota(jnp.int32, (chunk, chunk), 1)
    is_right = {b: (row & b) != 0 for b in levels}
    same_pair = {b: (rr // (2 * b)) == (cc // (2 * b)) for b in levels}
    sub_row = lax.broadcasted_iota(jnp.int32, (HGRN_SUB, hd), 0)
    wg = wg_ref[...]

    def chunk_body(c, carry):
        r0 = pl.multiple_of(c * chunk, chunk)
        rows = pl.ds(r0, chunk)
        for h in range(N_HEADS):
            cols = slice(h * hd, (h + 1) * hd)
            lb = lb_all[:, cols]
            q = hq_ref[0, 0, rows, cols]
            fa = f_ref[0, 0, rows, cols]
            v = v_ref[0, 0, rows, cols]
            f = lb + (1.0 - lb) * _sigmoid(fa)
            g = jnp.log(f)
            k = 1.0 - f
            cum = jnp.dot(tril, g, preferred_element_type=F32, precision=lax.Precision.HIGHEST)
            v_b = v.astype(BF16)

            o = jnp.zeros((chunk, hd), F32)
            if levels:
                a = jnp.zeros((chunk, chunk), F32)
                for b in levels:
                    pieces = []
                    for p in range(chunk // (2 * b)):
                        ref_row = cum[p * 2 * b + b - 1:p * 2 * b + b, :]
                        pieces.append(jnp.broadcast_to(ref_row, (2 * b, hd)))
                    ref = pieces[0] if len(pieces) == 1 else jnp.concatenate(pieces, axis=0)
                    e = jnp.exp(-jnp.abs(cum - ref))
                    ql = (q * jnp.where(is_right[b], e, 0.0)).astype(BF16)
                    kl = (k * jnp.where(is_right[b], 0.0, e)).astype(BF16)
                    al = lax.dot_general(ql, kl, (((1,), (1,)), ((), ())), preferred_element_type=F32)
                    a = a + jnp.where(same_pair[b], al, 0.0)
                o = jnp.dot(a.astype(BF16), v_b, preferred_element_type=F32)

            diag = []
            for j in range(chunk // HGRN_SUB):
                sl = slice(j * HGRN_SUB, (j + 1) * HGRN_SUB)
                qj, kj, cj, vj = q[sl], k[sl], cum[sl], v[sl]
                od = jnp.zeros((HGRN_SUB, hd), F32)
                for s in range(HGRN_SUB):
                    dec = jnp.exp(jnp.minimum(cj - cj[s:s + 1, :], 0.0))
                    dec = jnp.where(sub_row >= s, dec, 0.0)
                    a_s = jnp.sum(qj * dec * kj[s:s + 1, :], axis=-1, keepdims=True)
                    od = od + a_s * vj[s:s + 1, :]
                diag.append(od)
            o = o + (diag[0] if len(diag) == 1 else jnp.concatenate(diag, axis=0))

            st = st_scr[h]
            q0 = (q * jnp.exp(cum)).astype(BF16)
            o = o + lax.dot_general(q0, st.astype(BF16), (((1,), (1,)), ((), ())),
                                    preferred_element_type=F32)
            last = cum[chunk - 1:chunk, :]
            kd = (k * jnp.exp(last - cum)).astype(BF16)
            upd = jnp.dot(v.T.astype(BF16), kd, preferred_element_type=F32)
            st_scr[h] = st * jnp.exp(last) + upd

            o_ref[0, rows, cols] = _rms(o, wg) * sg_ref[0, 0, rows, cols]
        return carry

    lax.fori_loop(0, n_chunks, chunk_body, 0, unroll=2 if n_chunks % 2 == 0 else 1)

    @pl.when(t_blk == pl.num_programs(1) - 1)
    def _():
        for h in range(N_HEADS):
            sfin_ref[0, h] = st_scr[h].T


def _hgrn(proj, lb_logits, w_gnorm, s0, batch, seq):
    chunk = math.gcd(seq, HGRN_CHUNK)
    assert chunk % HGRN_SUB == 0
    tb = math.gcd(seq, 512)
    n_tb = seq // tb
    proj4 = proj.reshape(N_GROUPS, batch, seq, D_MODEL)
    has_s0 = s0 is not None
    if not has_s0:
        s0 = jnp.zeros((1, N_HEADS, HEAD_DIM, HEAD_DIM), F32)
    s0_map = (lambda b, t: (b, 0, 0, 0)) if has_s0 else (lambda b, t: (0, 0, 0, 0))
    n_lb = lb_logits.shape[0]

    def group(g):
        return pl.BlockSpec((1, 1, tb, D_MODEL), lambda b, t: (g, b, t, 0))

    o, s_fin = pl.pallas_call(
        functools.partial(_hgrn_kernel, chunk=chunk, n_chunks=tb // chunk, has_s0=has_s0),
        grid=(batch, n_tb),
        in_specs=[
            group(COL_HQ), group(COL_F), group(COL_I), group(COL_G),
            pl.BlockSpec((n_lb, D_MODEL), lambda b, t: (0, 0)),
            pl.BlockSpec((1, HEAD_DIM), lambda b, t: (0, 0)),
            pl.BlockSpec((1, N_HEADS, HEAD_DIM, HEAD_DIM), s0_map),
        ],
        out_specs=[
            pl.BlockSpec((1, tb, D_MODEL), lambda b, t: (b, t, 0)),
            pl.BlockSpec((1, N_HEADS, HEAD_DIM, HEAD_DIM), lambda b, t: (b, 0, 0, 0)),
        ],
        out_shape=[
            jax.ShapeDtypeStruct((batch, seq, D_MODEL), F32),
            jax.ShapeDtypeStruct((batch, N_HEADS, HEAD_DIM, HEAD_DIM), F32),
        ],
        scratch_shapes=[pltpu.VMEM((N_HEADS, HEAD_DIM, HEAD_DIM), F32)],
        compiler_params=_compiler_params(("arbitrary", "arbitrary")),
        name="hgrn2",
    )(proj4, proj4, proj4, proj4, lb_logits, w_gnorm, s0)
    return o.reshape(batch * seq, D_MODEL), s_fin


def _lambda_value(lam_ref, lam_init):
    l = lam_ref[...]
    a = jnp.sum(l[0:1, :] * l[1:2, :], axis=-1, keepdims=True)
    b = jnp.sum(l[2:3, :] * l[3:4, :], axis=-1, keepdims=True)
    return jnp.exp(a) - jnp.exp(b) + lam_init


def _split_maps(qh):
    lane = lax.broadcasted_iota(jnp.int32, qh.shape, 1)
    return jnp.concatenate([jnp.where(lane < QK_DIM, qh, 0.0), jnp.where(lane >= QK_DIM, qh, 0.0)], axis=0)


RING_SLOTS = 3
PAGES_PER_ITEM = 4


def _fused_attn_kernel(pt_ref, q_ref, kt_ref, v_ref, lam_ref, wsub_ref, qs_ref, kn_ref, vn_ref,
                       ck_hbm, cv_hbm, o_ref, os_ref,
                       ktb_scr, vb_scr, acc_scr, kbuf, vbuf, sem, sq_scr, sm_scr, sl_scr, sacc_scr,
                       *, tq, n_q, n_grid_steps, lam_init, n_new, page_size, groups_per_seq, n_items):
    hd = HEAD_DIM
    nrow = 2 * n_new
    scale = QK_DIM ** -0.5 * LOG2E
    step = pl.program_id(0) * pl.num_programs(1) + pl.program_id(1)
    kv_per_step = n_q * (n_q + 1) // 2
    lam = _lambda_value(lam_ref, lam_init)
    wsub = wsub_ref[...]

    def page_copies(item, lookup):
        slot = item % RING_SLOTS
        seq = item // groups_per_seq
        first_page = (item % groups_per_seq) * PAGES_PER_ITEM
        copies = []
        for i in range(PAGES_PER_ITEM):
            page = pt_ref[seq, first_page + i] if lookup else 0
            copies.append(pltpu.make_async_copy(ck_hbm.at[page], kbuf.at[slot, i], sem.at[slot]))
            copies.append(pltpu.make_async_copy(cv_hbm.at[page], vbuf.at[slot, i], sem.at[slot]))
        return copies

    def start_item(item):
        for c in page_copies(item, True):
            c.start()

    def wait_item(item):
        for c in page_copies(item, False):
            c.wait()

    def sample_update(scores, pv_fn, valid):
        s = jnp.concatenate(scores, axis=0)
        m_old, l_old, acc_old = sm_scr[...], sl_scr[...], sacc_scr[...]
        m_new = jnp.maximum(m_old, jnp.max(s, axis=-1, keepdims=True))
        alpha = jnp.exp2(m_old - m_new)
        p = jnp.exp2(s - m_new)
        l_new = alpha * l_old + jnp.sum(p, axis=-1, keepdims=True)
        pv = jnp.concatenate([pv_fn(h, p[h * nrow:(h + 1) * nrow, :]) for h in range(N_HEADS)], axis=0)
        acc_new = alpha * acc_old + pv
        if valid is not None:
            m_new = jnp.where(valid, m_new, m_old)
            l_new = jnp.where(valid, l_new, l_old)
            acc_new = jnp.where(valid, acc_new, acc_old)
        sm_scr[...] = m_new
        sl_scr[...] = l_new
        sacc_scr[...] = acc_new
        return l_new, acc_new

    def finish_sequence(seq):
        key = lax.broadcasted_iota(jnp.int32, (nrow, n_new), 1)
        qry = lax.broadcasted_iota(jnp.int32, (nrow, n_new), 0) % n_new
        q_f32 = sq_scr[...]
        kn = kn_ref[0, seq]
        vn = vn_ref[0, seq]
        scores = []
        for h in range(N_HEADS):
            s = lax.dot_general(q_f32[h * nrow:(h + 1) * nrow, :], kn[:, h * hd:(h + 1) * hd],
                                (((1,), (1,)), ((), ())), preferred_element_type=F32)
            scores.append(jnp.where(key <= qry, s, -jnp.inf))
        l_fin, acc_fin = sample_update(
            scores, lambda h, p: jnp.dot(p, vn[:, h * hd:(h + 1) * hd], preferred_element_type=F32), None)
        on = acc_fin / l_fin
        for h in range(N_HEADS):
            o = on[h * nrow:h * nrow + n_new, :] - lam * on[h * nrow + n_new:(h + 1) * nrow, :]
            os_ref[seq, :, h * hd:(h + 1) * hd] = _rms(o, wsub) * (1.0 - lam_init)

    def sample_item(item, valid, after_scores=None):
        def when(c, fn):
            if valid is not None:
                c = valid if c is None else jnp.logical_and(valid, c)
            if c is None:
                fn()
            else:
                pl.when(c)(fn)

        seq = item // groups_per_seq
        grp = item % groups_per_seq
        slot = item % RING_SLOTS

        def begin_sequence():
            q = qs_ref[0, seq] * scale
            for h in range(N_HEADS):
                sq_scr[h * nrow:(h + 1) * nrow, :] = _split_maps(q[:, h * hd:(h + 1) * hd])
            sm_scr[...] = jnp.full(sm_scr.shape, -jnp.inf, F32)
            sl_scr[...] = jnp.zeros(sl_scr.shape, F32)
            sacc_scr[...] = jnp.zeros(sacc_scr.shape, F32)

        def prime_ring():
            for first in range(min(RING_SLOTS - 1, n_items)):
                start_item(first)

        when(grp == 0, begin_sequence)
        when(item == 0, prime_ring)
        pl.when(item + (RING_SLOTS - 1) < n_items)(lambda: start_item(item + (RING_SLOTS - 1)))
        when(None, lambda: wait_item(item))

        q_all = sq_scr[...].astype(BF16)

        def head_keys(h):
            return jnp.concatenate([kbuf[slot, i, h * hd:(h + 1) * hd, :].astype(BF16)
                                    for i in range(PAGES_PER_ITEM)], axis=1)

        def head_values(h):
            return jnp.concatenate([vbuf.at[slot, i][pl.ds(h, page_size, stride=N_HEADS), :].astype(BF16)
                                    for i in range(PAGES_PER_ITEM)], axis=0)

        scores = [jnp.dot(q_all[h * nrow:(h + 1) * nrow, :], head_keys(h), preferred_element_type=F32)
                  for h in range(N_HEADS)]
        if after_scores is not None:
            after_scores()
        sample_update(scores,
                      lambda h, p: jnp.dot(p.astype(BF16), head_values(h), preferred_element_type=F32),
                      valid)
        return lambda: when(grp == groups_per_seq - 1, lambda: finish_sequence(seq))

    ktb_scr[...] = kt_ref[0].astype(BF16)
    vb_scr[...] = v_ref[0, 0].astype(BF16)
    tri = (lax.broadcasted_iota(jnp.int32, (2 * tq, tq), 1)
           <= (lax.broadcasted_iota(jnp.int32, (2 * tq, tq), 0) % tq))

    def q_body(qi, carry):
        q0 = pl.multiple_of(qi * tq, tq)
        qq = _split_maps(q_ref[0, 0, pl.ds(q0, tq), :] * scale).astype(BF16)
        acc_scr[...] = jnp.zeros((2 * tq, HEAD_DIM), F32)
        first_item = step * kv_per_step + (qi * (qi + 1)) // 2

        def kv_step(j, ml, masked):
            item = first_item + j
            m, l = ml
            k0 = pl.multiple_of(j * tq, tq)
            prompt_scores = []
            finish_item = sample_item(
                item, item < n_items,
                lambda: prompt_scores.append(
                    jnp.dot(qq, ktb_scr[:, pl.ds(k0, tq)], preferred_element_type=F32)))
            s = prompt_scores[0]
            if masked:
                s = jnp.where(tri, s, -jnp.inf)
            m_new = jnp.maximum(m, jnp.max(s, axis=-1, keepdims=True))
            alpha = jnp.exp2(m - m_new)
            p = jnp.exp2(s - m_new)
            l_new = alpha * l + jnp.sum(p, axis=-1, keepdims=True)
            acc_scr[...] = alpha * acc_scr[...] + jnp.dot(p.astype(BF16), vb_scr[pl.ds(k0, tq), :],
                                                          preferred_element_type=F32)
            finish_item()
            return m_new, l_new

        init = (jnp.full((2 * tq, 1), -jnp.inf, F32), jnp.zeros((2 * tq, 1), F32))
        ml = lax.fori_loop(0, qi, lambda j, ml: kv_step(j, ml, False), init)
        m, l = kv_step(qi, ml, True)
        on = acc_scr[...] / l
        o = on[0:tq, :] - lam * on[tq:2 * tq, :]
        o_ref[0, pl.ds(q0, tq), :] = _rms(o, wsub) * (1.0 - lam_init)
        return carry

    lax.fori_loop(0, n_q, q_body, 0)

    if n_grid_steps * kv_per_step < n_items:
        @pl.when(step == n_grid_steps - 1)
        def _():
            def tail(item, carry):
                sample_item(item, None)()
                return carry
            lax.fori_loop(n_grid_steps * kv_per_step, n_items, tail, 0)


def _fused_attn(proj_p, kt, proj_s, cache_k, cache_v, page_table, lam_params, w_subln,
                batch, seq, n_seq, n_new, lam_init):
    tq = math.gcd(seq, 512)
    n_q = seq // tq
    n_pages = page_table.shape[1]
    n_pool, page_size = cache_k.shape[0], cache_k.shape[1]
    assert n_new % 8 == 0 and n_pages % PAGES_PER_ITEM == 0
    groups_per_seq = n_pages // PAGES_PER_ITEM
    ckt = jnp.transpose(cache_k, (0, 2, 3, 4, 1)).reshape(n_pool, D_MODEL, page_size)
    cvr = cache_v.reshape(n_pool, page_size * N_HEADS, HEAD_DIM)
    proj4_p = proj_p.reshape(N_GROUPS, batch, seq, D_MODEL)
    proj4_s = proj_s.reshape(N_GROUPS, n_seq, n_new, D_MODEL)
    nrows = N_HEADS * 2 * n_new

    def sample_group(g):
        return pl.BlockSpec((1, n_seq, n_new, D_MODEL), lambda b, h, pt: (g, 0, 0, 0))

    grid_spec = pltpu.PrefetchScalarGridSpec(
        num_scalar_prefetch=1,
        grid=(batch, N_HEADS),
        in_specs=[pl.BlockSpec((1, 1, seq, HEAD_DIM), lambda b, h, pt: (COL_QB, b, 0, h)),
                  pl.BlockSpec((1, HEAD_DIM, seq), lambda b, h, pt: (b, h, 0)),
                  pl.BlockSpec((1, 1, seq, HEAD_DIM), lambda b, h, pt: (COL_VB, b, 0, h)),
                  pl.BlockSpec((4, QK_DIM), lambda b, h, pt: (0, 0)),
                  pl.BlockSpec((1, HEAD_DIM), lambda b, h, pt: (0, 0)),
                  sample_group(COL_QB), sample_group(COL_KB), sample_group(COL_VB),
                  pl.BlockSpec(memory_space=pl.ANY), pl.BlockSpec(memory_space=pl.ANY)],
        out_specs=[pl.BlockSpec((1, seq, HEAD_DIM), lambda b, h, pt: (b, 0, h)),
                   pl.BlockSpec((n_seq, n_new, D_MODEL), lambda b, h, pt: (0, 0, 0))],
        scratch_shapes=[pltpu.VMEM((HEAD_DIM, seq), BF16), pltpu.VMEM((seq, HEAD_DIM), BF16),
                        pltpu.VMEM((2 * tq, HEAD_DIM), F32),
                        pltpu.VMEM((RING_SLOTS, PAGES_PER_ITEM, D_MODEL, page_size), F32),
                        pltpu.VMEM((RING_SLOTS, PAGES_PER_ITEM, page_size * N_HEADS, HEAD_DIM), F32),
                        pltpu.SemaphoreType.DMA((RING_SLOTS,)),
                        pltpu.VMEM((nrows, HEAD_DIM), F32), pltpu.VMEM((nrows, 1), F32),
                        pltpu.VMEM((nrows, 1), F32), pltpu.VMEM((nrows, HEAD_DIM), F32)],
    )
    o_p, o_s = pl.pallas_call(
        functools.partial(_fused_attn_kernel, tq=tq, n_q=n_q, n_grid_steps=batch * N_HEADS,
                          lam_init=lam_init, n_new=n_new, page_size=page_size,
                          groups_per_seq=groups_per_seq, n_items=n_seq * groups_per_seq),
        grid_spec=grid_spec,
        out_shape=[jax.ShapeDtypeStruct((batch, seq, D_MODEL), F32),
                   jax.ShapeDtypeStruct((n_seq, n_new, D_MODEL), F32)],
        compiler_params=_compiler_params(("arbitrary", "arbitrary")),
        name="diff_attn_fused",
    )(page_table, proj4_p, kt, proj4_p, lam_params, w_subln, proj4_s, proj4_s, proj4_s, ckt, cvr)
    return o_p.reshape(batch * seq, D_MODEL), o_s.reshape(n_seq * n_new, D_MODEL)


def _merge_kernel(oa_ref, ob_ref, ga_ref, gb_ref, x_ref, wa_ref, wb_ref, wo_ref, g_ref, h_ref):
    ya = jnp.dot(oa_ref[...].astype(BF16), wa_ref[...], preferred_element_type=F32)
    yb = jnp.dot(ob_ref[...].astype(BF16), wb_ref[...], preferred_element_type=F32)
    y = ga_ref[0] * ya + gb_ref[0] * yb
    mix = jnp.dot(y.astype(BF16), wo_ref[...], preferred_element_type=F32)
    h_ref[...] = x_ref[...] + _rms(mix, g_ref[...])


def _merge(oa, ob, proj, x, wa, wb, wo, g_post, tm):
    n = x.shape[0]
    row = pl.BlockSpec((tm, D_MODEL), lambda i: (i, 0))
    wspec = pl.BlockSpec((D_MODEL, D_MODEL), lambda i: (0, 0))
    return pl.pallas_call(
        _merge_kernel,
        grid=(n // tm,),
        in_specs=[row, row,
                  pl.BlockSpec((1, tm, D_MODEL), lambda i: (COL_GA, i, 0)),
                  pl.BlockSpec((1, tm, D_MODEL), lambda i: (COL_GB, i, 0)),
                  row, wspec, wspec, wspec, pl.BlockSpec((1, D_MODEL), lambda i: (0, 0))],
        out_specs=row,
        out_shape=jax.ShapeDtypeStruct((n, D_MODEL), F32),
        compiler_params=_compiler_params(("arbitrary",)),
        name="merge_out",
    )(oa, ob, proj, proj, x, wa, wb, wo, g_post)


def _mem_attn_kernel(h_ref, mk_ref, mv_ref, gpre_ref, wq_ref, wo_ref, gpost_ref, o_ref, *, per_head_layout):
    h = h_ref[...]
    q = jnp.dot(_rms(h, gpre_ref[...]).astype(BF16), wq_ref[...], preferred_element_type=F32)
    q = (q * (MEM_DIM ** -0.5)).astype(BF16)
    outs = []
    for hh in range(MEM_HEADS):
        cols = slice(hh * MEM_DIM, (hh + 1) * MEM_DIM)
        if per_head_layout:
            mk, mv = mk_ref[0, :, hh, :], mv_ref[0, :, hh, :]
        else:
            mk, mv = mk_ref[0, :, cols], mv_ref[0, :, cols]
        s = lax.dot_general(q[:, cols], mk.astype(BF16), (((1,), (1,)), ((), ())),
                            preferred_element_type=F32)
        p = jnp.exp(s - jnp.max(s, axis=-1, keepdims=True))
        p = p / jnp.sum(p, axis=-1, keepdims=True)
        outs.append(jnp.dot(p.astype(BF16), mv.astype(BF16), preferred_element_type=F32))
    o = jnp.concatenate(outs, axis=-1).astype(BF16)
    mo = jnp.dot(o, wo_ref[...], preferred_element_type=F32)
    o_ref[...] = h + _rms(mo, gpost_ref[...])


def _mem_attn(h, mk, mv, g_pre, wq, wo, g_post, rows_per_batch, tm):
    n = h.shape[0]
    tiles_per_batch = rows_per_batch // tm
    per_head_layout = mk.ndim == 4
    row = pl.BlockSpec((tm, D_MODEL), lambda i: (i, 0))
    if per_head_layout:
        mem = pl.BlockSpec((1,) + mk.shape[1:], lambda i: (i // tiles_per_batch, 0, 0, 0))
    else:
        mem = pl.BlockSpec((1,) + mk.shape[1:], lambda i: (i // tiles_per_batch, 0, 0))
    wspec = pl.BlockSpec((D_MODEL, D_MODEL), lambda i: (0, 0))
    gspec = pl.BlockSpec((1, D_MODEL), lambda i: (0, 0))
    return pl.pallas_call(
        functools.partial(_mem_attn_kernel, per_head_layout=per_head_layout),
        grid=(n // tm,),
        in_specs=[row, mem, mem, gspec, wspec, wspec, gspec],
        out_specs=row,
        out_shape=jax.ShapeDtypeStruct((n, D_MODEL), F32),
        compiler_params=_compiler_params(("arbitrary",)),
        name="mem_attn",
    )(h, mk, mv, g_pre, wq, wo, g_post)


def _mlp_kernel(h_ref, gpre_ref, wup_ref, wdown_ref, gpost_ref, o_ref, xn_scr, acc_scr):
    j = pl.program_id(1)

    @pl.when(j == 0)
    def _():
        xn_scr[...] = _rms(h_ref[...], gpre_ref[...]).astype(BF16)
        acc_scr[...] = jnp.zeros_like(acc_scr)

    u = jnp.dot(xn_scr[...], wup_ref[...], preferred_element_type=F32)
    u = jnp.square(jnp.maximum(u, 0.0)).astype(BF16)
    acc_scr[...] += jnp.dot(u, wdown_ref[...], preferred_element_type=F32)

    @pl.when(j == pl.num_programs(1) - 1)
    def _():
        o_ref[...] = h_ref[...] + _rms(acc_scr[...], gpost_ref[...])


def _mlp(h, g_pre, wup, wdown, g_post, tm):
    n = h.shape[0]
    tf = D_MODEL
    row = pl.BlockSpec((tm, D_MODEL), lambda i, j: (i, 0))
    gspec = pl.BlockSpec((1, D_MODEL), lambda i, j: (0, 0))
    return pl.pallas_call(
        _mlp_kernel,
        grid=(n // tm, D_FF // tf),
        in_specs=[row, gspec,
                  pl.BlockSpec((D_MODEL, tf), lambda i, j: (0, j)),
                  pl.BlockSpec((tf, D_MODEL), lambda i, j: (j, 0)),
                  gspec],
        out_specs=row,
        out_shape=jax.ShapeDtypeStruct((n, D_MODEL), F32),
        scratch_shapes=[pltpu.VMEM((tm, D_MODEL), BF16), pltpu.VMEM((tm, D_MODEL), F32)],
        compiler_params=_compiler_params(("arbitrary", "arbitrary")),
        name="mlp",
    )(h, g_pre, wup, wdown, g_post)


def _post_attn(o_a, o_b, proj, x, seq, mem_kv, p):
    tm = math.gcd(x.shape[0], 512)
    h = _merge(o_a, o_b, proj, x, p["w_branch_a"], p["w_branch_b"], p["w_out"], p["w_post_mix"], tm)
    mk, mv = mem_kv
    h = _mem_attn(h, mk, mv, p["w_pre_mem"], p["w_mq"], p["w_mo"], p["w_post_mem"], seq,
                  math.gcd(seq, 512))
    return _mlp(h, p["w_pre_mlp"], p["w_up"], p["w_down"], p["w_post_mlp"], tm)


def kernel(x_prompt, x_sample, cache_attn_k, cache_attn_v, state_hgrn, cache_mem_k, cache_mem_v,
           page_table, mem_prompt, w_pre_mix, w_in, hgrn_lb_logits, w_hgrn_gnorm,
           lambda_q1, lambda_k1, lambda_q2, lambda_k2, w_subln, w_branch_a, w_branch_b, w_out,
           w_post_mix, w_pre_mem, w_mem_norm, w_mq, w_mk, w_mv, w_mo, w_post_mem,
           w_pre_mlp, w_up, w_down, w_post_mlp):
    depth = w_in.shape[0]
    assert depth == 1, "single-layer step"
    l = 0
    batch, seq, _ = x_prompt.shape
    n_seq, n_new, _ = x_sample.shape
    mem_len = mem_prompt.shape[1]
    lam_init = 0.8 - 0.6 * math.exp(-0.3 * l)

    def gain(w):
        return w[l].reshape(1, -1)

    def weight(w):
        return w[l].astype(BF16)

    p = {
        "w_pre_mix": gain(w_pre_mix),
        "w_in": jnp.transpose(weight(w_in).reshape(D_MODEL, N_GROUPS, D_MODEL), (1, 0, 2)),
        "lb_logits": hgrn_lb_logits,
        "w_hgrn_gnorm": gain(w_hgrn_gnorm), "w_subln": gain(w_subln),
        "lam_params": jnp.stack([lambda_q1[l], lambda_k1[l], lambda_q2[l], lambda_k2[l]]),
        "w_branch_a": weight(w_branch_a), "w_branch_b": weight(w_branch_b), "w_out": weight(w_out),
        "w_post_mix": gain(w_post_mix), "w_pre_mem": gain(w_pre_mem), "w_mq": weight(w_mq),
        "w_mo": weight(w_mo), "w_post_mem": gain(w_post_mem), "w_pre_mlp": gain(w_pre_mlp),
        "w_up": weight(w_up), "w_down": weight(w_down), "w_post_mlp": gain(w_post_mlp),
    }

    memx = mem_prompt.reshape(batch * mem_len, D_MODEL)
    tm_mem = math.gcd(batch * mem_len, 512)
    mk_p = _norm_matmul(memx, gain(w_mem_norm), weight(w_mk), tm_mem).reshape(batch, mem_len, D_MODEL)
    mv_p = _norm_matmul(memx, gain(w_mem_norm), weight(w_mv), tm_mem).reshape(batch, mem_len, D_MODEL)
    xp = x_prompt.reshape(batch * seq, D_MODEL)
    proj_p, kt_p = _in_proj(xp, p["w_pre_mix"], p["w_in"], batch, seq, transpose_k=True)
    oa_p, s_p = _hgrn(proj_p, p["lb_logits"], p["w_hgrn_gnorm"], None, batch, seq)

    xs = x_sample.reshape(n_seq * n_new, D_MODEL)
    proj_s, _ = _in_proj(xs, p["w_pre_mix"], p["w_in"], n_seq, n_new, transpose_k=False)
    oa_s, s_s = _hgrn(proj_s, p["lb_logits"], p["w_hgrn_gnorm"], state_hgrn[l], n_seq, n_new)

    ob_p, ob_s = _fused_attn(proj_p, kt_p, proj_s, cache_attn_k[l], cache_attn_v[l], page_table,
                             p["lam_params"], p["w_subln"], batch, seq, n_seq, n_new, lam_init)
    hp = _post_attn(oa_p, ob_p, proj_p, xp, seq, (mk_p, mv_p), p)
    hs = _post_attn(oa_s, ob_s, proj_s, xs, n_new, (cache_mem_k[l], cache_mem_v[l]), p)
    k_p = jnp.transpose(kt_p.reshape(batch, N_HEADS, 2, QK_DIM, seq), (0, 4, 1, 2, 3))
    v_p = proj_p[COL_VB].reshape(batch, seq, N_HEADS, HEAD_DIM)
    k_s = proj_s[COL_KB].reshape(n_seq, n_new, N_HEADS, 2, QK_DIM)
    v_s = proj_s[COL_VB].reshape(n_seq, n_new, N_HEADS, HEAD_DIM)

    return (hp.reshape(batch, seq, D_MODEL), hs.reshape(n_seq, n_new, D_MODEL),
            k_p[None], v_p[None], s_p[None],
            mk_p.reshape(1, batch, mem_len, MEM_HEADS, MEM_DIM),
            mv_p.reshape(1, batch, mem_len, MEM_HEADS, MEM_DIM),
            k_s[None], v_s[None], s_s[None])
```

```python
import functools
import math

import jax
import jax.numpy as jnp
from jax import lax
from jax.experimental import pallas as pl
from jax.experimental.pallas import tpu as pltpu

F32 = jnp.float32
BF16 = jnp.bfloat16

D_MODEL = 1024
N_HEADS = 8
HEAD_DIM = 128
QK_DIM = 64
MEM_HEADS = 4
MEM_DIM = 256
D_FF = 4 * D_MODEL
EPS = 1e-6
LOG2E = math.log2(math.e)
HGRN_CHUNK = 64
HGRN_SUB = 8
VMEM_LIMIT_BYTES = 56 * 1024 * 1024

COL_HQ, COL_F, COL_I, COL_G, COL_QB, COL_KB, COL_VB, COL_GA, COL_GB = range(9)
N_GROUPS = 9


def _sigmoid(x):
    return 1.0 / (1.0 + jnp.exp(-x))


def _rms(x, gain):
    ms = jnp.mean(x * x, axis=-1, keepdims=True)
    return x * lax.rsqrt(ms + EPS) * gain


def _compiler_params(semantics):
    return pltpu.CompilerParams(dimension_semantics=semantics, vmem_limit_bytes=VMEM_LIMIT_BYTES)


def _norm_matmul_kernel(x_ref, g_ref, w_ref, o_ref):
    xn = _rms(x_ref[...], g_ref[...]).astype(BF16)
    o_ref[...] = jnp.dot(xn, w_ref[...], preferred_element_type=F32)


def _norm_matmul(x, gain, w, tm):
    n, d = x.shape
    return pl.pallas_call(
        _norm_matmul_kernel,
        grid=(n // tm,),
        in_specs=[
            pl.BlockSpec((tm, d), lambda i: (i, 0)),
            pl.BlockSpec((1, d), lambda i: (0, 0)),
            pl.BlockSpec((d, D_MODEL), lambda i: (0, 0)),
        ],
        out_specs=pl.BlockSpec((tm, D_MODEL), lambda i: (i, 0)),
        out_shape=jax.ShapeDtypeStruct((n, D_MODEL), F32),
        compiler_params=_compiler_params(("arbitrary",)),
        name="norm_matmul",
    )(x, gain, w)


def _in_proj_kernel(x_ref, g_ref, w_ref, o_ref, *rest, transpose_k):
    if transpose_k:
        kt_ref, xn_scr = rest
    else:
        (xn_scr,) = rest
    j = pl.program_id(1)

    @pl.when(j == 0)
    def _():
        xn_scr[...] = _rms(x_ref[...], g_ref[...]).astype(BF16)

    y = jnp.dot(xn_scr[...], w_ref[j], preferred_element_type=F32)
    is_silu = (j == COL_HQ) | (j == COL_G)
    is_sigmoid = (j == COL_GA) | (j == COL_GB)
    sig = 0.5 * jnp.tanh(0.5 * y) + 0.5
    o_ref[0] = jnp.where(is_silu, y * sig, jnp.where(is_sigmoid, sig, y))

    if transpose_k:
        @pl.when(j == COL_KB)
        def _():
            kt_ref[0] = y.T


def _in_proj(x, gain, w_in, batch, seq, transpose_k):
    n = x.shape[0]
    tm = math.gcd(seq, 512) if transpose_k else math.gcd(n, 512)
    tiles_per_batch = seq // tm if transpose_k else 1
    out_specs = [pl.BlockSpec((1, tm, D_MODEL), lambda i, j: (j, i, 0))]
    out_shape = [jax.ShapeDtypeStruct((N_GROUPS, n, D_MODEL), F32)]
    if transpose_k:
        out_specs.append(pl.BlockSpec((1, D_MODEL, tm),
                                      lambda i, j: (i // tiles_per_batch, 0, i % tiles_per_batch)))
        out_shape.append(jax.ShapeDtypeStruct((batch, D_MODEL, seq), F32))
    res = pl.pallas_call(
        functools.partial(_in_proj_kernel, transpose_k=transpose_k),
        grid=(n // tm, N_GROUPS),
        in_specs=[
            pl.BlockSpec((tm, D_MODEL), lambda i, j: (i, 0)),
            pl.BlockSpec((1, D_MODEL), lambda i, j: (0, 0)),
            pl.BlockSpec((N_GROUPS, D_MODEL, D_MODEL), lambda i, j: (0, 0, 0),
                         pipeline_mode=pl.Buffered(1)),
        ],
        out_specs=out_specs,
        out_shape=out_shape,
        scratch_shapes=[pltpu.VMEM((tm, D_MODEL), BF16)],
        compiler_params=_compiler_params(("arbitrary", "arbitrary")),
        name="in_proj",
    )(x, gain, w_in)
    return (res[0], res[1]) if transpose_k else (res[0], None)


def _hgrn_levels(chunk):
    levels = []
    b = chunk // 2
    while b >= HGRN_SUB:
        levels.append(b)
        b //= 2
    return levels


def _hgrn_kernel(hq_ref, f_ref, v_ref, sg_ref, lbl_ref, wg_ref, s0_ref, o_ref, sfin_ref, st_scr,
                 *, chunk, n_chunks, has_s0):
    t_blk = pl.program_id(1)
    hd = HEAD_DIM

    @pl.when(t_blk == 0)
    def _():
        for h in range(N_HEADS):
            if has_s0:
                st_scr[h] = s0_ref[0, h].T
            else:
                st_scr[h] = jnp.zeros((hd, hd), F32)

    lbl = lbl_ref[...]
    lmax = jnp.max(lbl, axis=0, keepdims=True)
    lexp = jnp.exp(lbl - lmax)
    lb_all = lexp[0:1, :] / jnp.sum(lexp, axis=0, keepdims=True)

    levels = _hgrn_levels(chunk)
    row = lax.broadcasted_iota(jnp.int32, (chunk, hd), 0)
    tril = (lax.broadcasted_iota(jnp.int32, (chunk, chunk), 0)
            >= lax.broadcasted_iota(jnp.int32, (chunk, chunk), 1)).astype(F32)
    rr = lax.broadcasted_iota(jnp.int32, (chunk, chunk), 0)
    cc = lax.broadcasted_iota(jnp.int32, (chunk, chunk), 1)
    is_right = {b: (row & b) != 0 for b in levels}
    same_pair = {b: (rr // (2 * b)) == (cc // (2 * b)) for b in levels}
    sub_row = lax.broadcasted_iota(jnp.int32, (HGRN_SUB, hd), 0)
    wg = wg_ref[...]

    def chunk_body(c, carry):
        r0 = pl.multiple_of(c * chunk, chunk)
        rows = pl.ds(r0, chunk)
        for h in range(N_HEADS):
            cols = slice(h * hd, (h + 1) * hd)
            lb = lb_all[:, cols]
            q = hq_ref[0, 0, rows, cols]
            fa = f_ref[0, 0, rows, cols]
            v = v_ref[0, 0, rows, cols]
            f = lb + (1.0 - lb) * _sigmoid(fa)
            g = jnp.log(f)
            k = 1.0 - f
            cum = jnp.dot(tril, g, preferred_element_type=F32, precision=lax.Precision.HIGHEST)
            v_b = v.astype(BF16)

            o = jnp.zeros((chunk, hd), F32)
            if levels:
                a = jnp.zeros((chunk, chunk), F32)
                for b in levels:
                    pieces = []
                    for p in range(chunk // (2 * b)):
                        ref_row = cum[p * 2 * b + b - 1:p * 2 * b + b, :]
                        pieces.append(jnp.broadcast_to(ref_row, (2 * b, hd)))
                    ref = pieces[0] if len(pieces) == 1 else jnp.concatenate(pieces, axis=0)
                    e = jnp.exp(-jnp.abs(cum - ref))
                    ql = (q * jnp.where(is_right[b], e, 0.0)).astype(BF16)
                    kl = (k * jnp.where(is_right[b], 0.0, e)).astype(BF16)
                    al = lax.dot_general(ql, kl, (((1,), (1,)), ((), ())), preferred_element_type=F32)
                    a = a + jnp.where(same_pair[b], al, 0.0)
                o = jnp.dot(a.astype(BF16), v_b, preferred_element_type=F32)

            diag = []
            for j in range(chunk // HGRN_SUB):
                sl = slice(j * HGRN_SUB, (j + 1) * HGRN_SUB)
                qj, kj, cj, vj = q[sl], k[sl], cum[sl], v[sl]
                od = jnp.zeros((HGRN_SUB, hd), F32)
                for s in range(HGRN_SUB):
                    dec = jnp.exp(jnp.minimum(cj - cj[s:s + 1, :], 0.0))
                    dec = jnp.where(sub_row >= s, dec, 0.0)
                    a_s = jnp.sum(qj * dec * kj[s:s + 1, :], axis=-1, keepdims=True)
                    od = od + a_s * vj[s:s + 1, :]
                diag.append(od)
            o = o + (diag[0] if len(diag) == 1 else jnp.concatenate(diag, axis=0))

            st = st_scr[h]
            q0 = (q * jnp.exp(cum)).astype(BF16)
            o = o + lax.dot_general(q0, st.astype(BF16), (((1,), (1,)), ((), ())),
                                    preferred_element_type=F32)
            last = cum[chunk - 1:chunk, :]
            kd = (k * jnp.exp(last - cum)).astype(BF16)
            upd = jnp.dot(v.T.astype(BF16), kd, preferred_element_type=F32)
            st_scr[h] = st * jnp.exp(last) + upd

            o_ref[0, rows, cols] = _rms(o, wg) * sg_ref[0, 0, rows, cols]
        return carry

    lax.fori_loop(0, n_chunks, chunk_body, 0, unroll=math.gcd(n_chunks, 4))

    @pl.when(t_blk == pl.num_programs(1) - 1)
    def _():
        for h in range(N_HEADS):
            sfin_ref[0, h] = st_scr[h].T


def _hgrn(proj, lb_logits, w_gnorm, s0, batch, seq):
    chunk = math.gcd(seq, HGRN_CHUNK)
    assert chunk % HGRN_SUB == 0
    tb = math.gcd(seq, 512)
    n_tb = seq // tb
    proj4 = proj.reshape(N_GROUPS, batch, seq, D_MODEL)
    has_s0 = s0 is not None
    if not has_s0:
        s0 = jnp.zeros((1, N_HEADS, HEAD_DIM, HEAD_DIM), F32)
    s0_map = (lambda b, t: (b, 0, 0, 0)) if has_s0 else (lambda b, t: (0, 0, 0, 0))
    n_lb = lb_logits.shape[0]

    def group(g):
        return pl.BlockSpec((1, 1, tb, D_MODEL), lambda b, t: (g, b, t, 0))

    o, s_fin = pl.pallas_call(
        functools.partial(_hgrn_kernel, chunk=chunk, n_chunks=tb // chunk, has_s0=has_s0),
        grid=(batch, n_tb),
        in_specs=[
            group(COL_HQ), group(COL_F), group(COL_I), group(COL_G),
            pl.BlockSpec((n_lb, D_MODEL), lambda b, t: (0, 0)),
            pl.BlockSpec((1, HEAD_DIM), lambda b, t: (0, 0)),
            pl.BlockSpec((1, N_HEADS, HEAD_DIM, HEAD_DIM), s0_map),
        ],
        out_specs=[
            pl.BlockSpec((1, tb, D_MODEL), lambda b, t: (b, t, 0)),
            pl.BlockSpec((1, N_HEADS, HEAD_DIM, HEAD_DIM), lambda b, t: (b, 0, 0, 0)),
        ],
        out_shape=[
            jax.ShapeDtypeStruct((batch, seq, D_MODEL), F32),
            jax.ShapeDtypeStruct((batch, N_HEADS, HEAD_DIM, HEAD_DIM), F32),
        ],
        scratch_shapes=[pltpu.VMEM((N_HEADS, HEAD_DIM, HEAD_DIM), F32)],
        compiler_params=_compiler_params(("arbitrary", "arbitrary")),
        name="hgrn2",
    )(proj4, proj4, proj4, proj4, lb_logits, w_gnorm, s0)
    return o.reshape(batch * seq, D_MODEL), s_fin


def _lambda_value(lam_ref, lam_init):
    l = lam_ref[...]
    a = jnp.sum(l[0:1, :] * l[1:2, :], axis=-1, keepdims=True)
    b = jnp.sum(l[2:3, :] * l[3:4, :], axis=-1, keepdims=True)
    return jnp.exp(a) - jnp.exp(b) + lam_init


def _split_maps(qh):
    lane = lax.broadcasted_iota(jnp.int32, qh.shape, 1)
    return jnp.concatenate([jnp.where(lane < QK_DIM, qh, 0.0), jnp.where(lane >= QK_DIM, qh, 0.0)], axis=0)


RING_SLOTS = 3
PAGES_PER_ITEM = 4


def _fused_attn_kernel(pt_ref, q_ref, kt_ref, v_ref, lam_ref, wsub_ref, qs_ref, kn_ref, vn_ref,
                       ck_hbm, cv_hbm, o_ref, os_ref,
                       ktb_scr, vb_scr, acc_scr, kbuf, vbuf, sem, sq_scr, sm_scr, sl_scr, sacc_scr,
                       *, tq, n_q, n_grid_steps, lam_init, n_new, page_size, groups_per_seq, n_items):
    hd = HEAD_DIM
    nrow = 2 * n_new
    scale = QK_DIM ** -0.5 * LOG2E
    step = pl.program_id(0) * pl.num_programs(1) + pl.program_id(1)
    kv_per_step = n_q * (n_q + 1) // 2
    lam = _lambda_value(lam_ref, lam_init)
    wsub = wsub_ref[...]

    def page_copies(item, lookup):
        slot = item % RING_SLOTS
        seq = item // groups_per_seq
        first_page = (item % groups_per_seq) * PAGES_PER_ITEM
        copies = []
        for i in range(PAGES_PER_ITEM):
            page = pt_ref[seq, first_page + i] if lookup else 0
            copies.append(pltpu.make_async_copy(ck_hbm.at[page], kbuf.at[slot, i], sem.at[slot]))
            copies.append(pltpu.make_async_copy(cv_hbm.at[page], vbuf.at[slot, i], sem.at[slot]))
        return copies

    def start_item(item):
        for c in page_copies(item, True):
            c.start()

    def wait_item(item):
        for c in page_copies(item, False):
            c.wait()

    def sample_update(scores, pv_fn, valid):
        s = jnp.concatenate(scores, axis=0)
        m_old, l_old, acc_old = sm_scr[...], sl_scr[...], sacc_scr[...]
        m_new = jnp.maximum(m_old, jnp.max(s, axis=-1, keepdims=True))
        alpha = jnp.exp2(m_old - m_new)
        p = jnp.exp2(s - m_new)
        l_new = alpha * l_old + jnp.sum(p, axis=-1, keepdims=True)
        pv = jnp.concatenate([pv_fn(h, p[h * nrow:(h + 1) * nrow, :]) for h in range(N_HEADS)], axis=0)
        acc_new = alpha * acc_old + pv
        if valid is not None:
            m_new = jnp.where(valid, m_new, m_old)
            l_new = jnp.where(valid, l_new, l_old)
            acc_new = jnp.where(valid, acc_new, acc_old)
        sm_scr[...] = m_new
        sl_scr[...] = l_new
        sacc_scr[...] = acc_new
        return l_new, acc_new

    def finish_sequence(seq):
        key = lax.broadcasted_iota(jnp.int32, (nrow, n_new), 1)
        qry = lax.broadcasted_iota(jnp.int32, (nrow, n_new), 0) % n_new
        q_f32 = sq_scr[...]
        kn = kn_ref[0, seq]
        vn = vn_ref[0, seq]
        scores = []
        for h in range(N_HEADS):
            s = lax.dot_general(q_f32[h * nrow:(h + 1) * nrow, :], kn[:, h * hd:(h + 1) * hd],
                                (((1,), (1,)), ((), ())), preferred_element_type=F32)
            scores.append(jnp.where(key <= qry, s, -jnp.inf))
        l_fin, acc_fin = sample_update(
            scores, lambda h, p: jnp.dot(p, vn[:, h * hd:(h + 1) * hd], preferred_element_type=F32), None)
        on = acc_fin / l_fin
        for h in range(N_HEADS):
            o = on[h * nrow:h * nrow + n_new, :] - lam * on[h * nrow + n_new:(h + 1) * nrow, :]
            os_ref[seq, :, h * hd:(h + 1) * hd] = _rms(o, wsub) * (1.0 - lam_init)

    def sample_item(item, valid, after_scores=None):
        def when(c, fn):
            if valid is not None:
                c = valid if c is None else jnp.logical_and(valid, c)
            if c is None:
                fn()
            else:
                pl.when(c)(fn)

        seq = item // groups_per_seq
        grp = item % groups_per_seq
        slot = item % RING_SLOTS

        def begin_sequence():
            q = qs_ref[0, seq] * scale
            for h in range(N_HEADS):
                sq_scr[h * nrow:(h + 1) * nrow, :] = _split_maps(q[:, h * hd:(h + 1) * hd])
            sm_scr[...] = jnp.full(sm_scr.shape, -jnp.inf, F32)
            sl_scr[...] = jnp.zeros(sl_scr.shape, F32)
            sacc_scr[...] = jnp.zeros(sacc_scr.shape, F32)

        def prime_ring():
            for first in range(min(RING_SLOTS - 1, n_items)):
                start_item(first)

        when(grp == 0, begin_sequence)
        when(item == 0, prime_ring)
        pl.when(item + (RING_SLOTS - 1) < n_items)(lambda: start_item(item + (RING_SLOTS - 1)))
        when(None, lambda: wait_item(item))

        q_all = sq_scr[...].astype(BF16)

        def head_keys(h):
            return jnp.concatenate([kbuf[slot, i, h * hd:(h + 1) * hd, :].astype(BF16)
                                    for i in range(PAGES_PER_ITEM)], axis=1)

        def head_values(h):
            return jnp.concatenate([vbuf.at[slot, i][pl.ds(h, page_size, stride=N_HEADS), :].astype(BF16)
                                    for i in range(PAGES_PER_ITEM)], axis=0)

        scores = [jnp.dot(q_all[h * nrow:(h + 1) * nrow, :], head_keys(h), preferred_element_type=F32)
                  for h in range(N_HEADS)]
        if after_scores is not None:
            after_scores()
        sample_update(scores,
                      lambda h, p: jnp.dot(p.astype(BF16), head_values(h), preferred_element_type=F32),
                      valid)
        return lambda: when(grp == groups_per_seq - 1, lambda: finish_sequence(seq))

    ktb_scr[...] = kt_ref[0].astype(BF16)
    vb_scr[...] = v_ref[0, 0].astype(BF16)
    tri = (lax.broadcasted_iota(jnp.int32, (2 * tq, tq), 1)
           <= (lax.broadcasted_iota(jnp.int32, (2 * tq, tq), 0) % tq))

    def q_body(qi, carry):
        q0 = pl.multiple_of(qi * tq, tq)
        qq = _split_maps(q_ref[0, 0, pl.ds(q0, tq), :] * scale).astype(BF16)
        acc_scr[...] = jnp.zeros((2 * tq, HEAD_DIM), F32)
        first_item = step * kv_per_step + (qi * (qi + 1)) // 2

        def kv_step(j, ml, masked, with_items):
            item = first_item + j
            m, l = ml
            k0 = pl.multiple_of(j * tq, tq)
            prompt_scores = []

            def flash_scores():
                prompt_scores.append(jnp.dot(qq, ktb_scr[:, pl.ds(k0, tq)], preferred_element_type=F32))

            if with_items:
                finish_item = sample_item(item, item < n_items, flash_scores)
            else:
                flash_scores()
            s = prompt_scores[0]
            if masked:
                s = jnp.where(tri, s, -jnp.inf)
            m_new = jnp.maximum(m, jnp.max(s, axis=-1, keepdims=True))
            alpha = jnp.exp2(m - m_new)
            p = jnp.exp2(s - m_new)
            l_new = alpha * l + jnp.sum(p, axis=-1, keepdims=True)
            acc_scr[...] = alpha * acc_scr[...] + jnp.dot(p.astype(BF16), vb_scr[pl.ds(k0, tq), :],
                                                          preferred_element_type=F32)
            if with_items:
                finish_item()
            return m_new, l_new

        def flash_row(with_items):
            def run():
                init = (jnp.full((2 * tq, 1), -jnp.inf, F32), jnp.zeros((2 * tq, 1), F32))
                ml = lax.fori_loop(0, qi, lambda j, ml: kv_step(j, ml, False, with_items), init)
                return kv_step(qi, ml, True, with_items)
            return run

        m, l = lax.cond(first_item < n_items, flash_row(True), flash_row(False))
        on = acc_scr[...] / l
        o = on[0:tq, :] - lam * on[tq:2 * tq, :]
        o_ref[0, pl.ds(q0, tq), :] = _rms(o, wsub) * (1.0 - lam_init)
        return carry

    lax.fori_loop(0, n_q, q_body, 0)

    if n_grid_steps * kv_per_step < n_items:
        @pl.when(step == n_grid_steps - 1)
        def _():
            def tail(item, carry):
                sample_item(item, None)()
                return carry
            lax.fori_loop(n_grid_steps * kv_per_step, n_items, tail, 0)


def _fused_attn(proj_p, kt, proj_s, cache_k, cache_v, page_table, lam_params, w_subln,
                batch, seq, n_seq, n_new, lam_init):
    tq = math.gcd(seq, 512)
    n_q = seq // tq
    n_pages = page_table.shape[1]
    n_pool, page_size = cache_k.shape[0], cache_k.shape[1]
    assert n_new % 8 == 0 and n_pages % PAGES_PER_ITEM == 0
    groups_per_seq = n_pages // PAGES_PER_ITEM
    ckt = jnp.transpose(cache_k, (0, 2, 3, 4, 1)).reshape(n_pool, D_MODEL, page_size)
    cvr = cache_v.reshape(n_pool, page_size * N_HEADS, HEAD_DIM)
    proj4_p = proj_p.reshape(N_GROUPS, batch, seq, D_MODEL)
    proj4_s = proj_s.reshape(N_GROUPS, n_seq, n_new, D_MODEL)
    nrows = N_HEADS * 2 * n_new

    def sample_group(g):
        return pl.BlockSpec((1, n_seq, n_new, D_MODEL), lambda b, h, pt: (g, 0, 0, 0))

    grid_spec = pltpu.PrefetchScalarGridSpec(
        num_scalar_prefetch=1,
        grid=(batch, N_HEADS),
        in_specs=[pl.BlockSpec((1, 1, seq, HEAD_DIM), lambda b, h, pt: (COL_QB, b, 0, h)),
                  pl.BlockSpec((1, HEAD_DIM, seq), lambda b, h, pt: (b, h, 0)),
                  pl.BlockSpec((1, 1, seq, HEAD_DIM), lambda b, h, pt: (COL_VB, b, 0, h)),
                  pl.BlockSpec((4, QK_DIM), lambda b, h, pt: (0, 0)),
                  pl.BlockSpec((1, HEAD_DIM), lambda b, h, pt: (0, 0)),
                  sample_group(COL_QB), sample_group(COL_KB), sample_group(COL_VB),
                  pl.BlockSpec(memory_space=pl.ANY), pl.BlockSpec(memory_space=pl.ANY)],
        out_specs=[pl.BlockSpec((1, seq, HEAD_DIM), lambda b, h, pt: (b, 0, h)),
                   pl.BlockSpec((n_seq, n_new, D_MODEL), lambda b, h, pt: (0, 0, 0))],
        scratch_shapes=[pltpu.VMEM((HEAD_DIM, seq), BF16), pltpu.VMEM((seq, HEAD_DIM), BF16),
                        pltpu.VMEM((2 * tq, HEAD_DIM), F32),
                        pltpu.VMEM((RING_SLOTS, PAGES_PER_ITEM, D_MODEL, page_size), F32),
                        pltpu.VMEM((RING_SLOTS, PAGES_PER_ITEM, page_size * N_HEADS, HEAD_DIM), F32),
                        pltpu.SemaphoreType.DMA((RING_SLOTS,)),
                        pltpu.VMEM((nrows, HEAD_DIM), F32), pltpu.VMEM((nrows, 1), F32),
                        pltpu.VMEM((nrows, 1), F32), pltpu.VMEM((nrows, HEAD_DIM), F32)],
    )
    o_p, o_s = pl.pallas_call(
        functools.partial(_fused_attn_kernel, tq=tq, n_q=n_q, n_grid_steps=batch * N_HEADS,
                          lam_init=lam_init, n_new=n_new, page_size=page_size,
                          groups_per_seq=groups_per_seq, n_items=n_seq * groups_per_seq),
        grid_spec=grid_spec,
        out_shape=[jax.ShapeDtypeStruct((batch, seq, D_MODEL), F32),
                   jax.ShapeDtypeStruct((n_seq, n_new, D_MODEL), F32)],
        compiler_params=_compiler_params(("arbitrary", "arbitrary")),
        name="diff_attn_fused",
    )(page_table, proj4_p, kt, proj4_p, lam_params, w_subln, proj4_s, proj4_s, proj4_s, ckt, cvr)
    return o_p.reshape(batch * seq, D_MODEL), o_s.reshape(n_seq * n_new, D_MODEL)


def _merge_kernel(oa_ref, ob_ref, ga_ref, gb_ref, x_ref, wa_ref, wb_ref, wo_ref, g_ref, h_ref):
    ya = jnp.dot(oa_ref[...].astype(BF16), wa_ref[...], preferred_element_type=F32)
    yb = jnp.dot(ob_ref[...].astype(BF16), wb_ref[...], preferred_element_type=F32)
    y = ga_ref[0] * ya + gb_ref[0] * yb
    mix = jnp.dot(y.astype(BF16), wo_ref[...], preferred_element_type=F32)
    h_ref[...] = x_ref[...] + _rms(mix, g_ref[...])


def _merge(oa, ob, proj, x, wa, wb, wo, g_post, tm):
    n = x.shape[0]
    row = pl.BlockSpec((tm, D_MODEL), lambda i: (i, 0))
    wspec = pl.BlockSpec((D_MODEL, D_MODEL), lambda i: (0, 0))
    return pl.pallas_call(
        _merge_kernel,
        grid=(n // tm,),
        in_specs=[row, row,
                  pl.BlockSpec((1, tm, D_MODEL), lambda i: (COL_GA, i, 0)),
                  pl.BlockSpec((1, tm, D_MODEL), lambda i: (COL_GB, i, 0)),
                  row, wspec, wspec, wspec, pl.BlockSpec((1, D_MODEL), lambda i: (0, 0))],
        out_specs=row,
        out_shape=jax.ShapeDtypeStruct((n, D_MODEL), F32),
        compiler_params=_compiler_params(("arbitrary",)),
        name="merge_out",
    )(oa, ob, proj, proj, x, wa, wb, wo, g_post)


def _mem_attn_kernel(h_ref, mk_ref, mv_ref, gpre_ref, wq_ref, wo_ref, gpost_ref, o_ref, *, per_head_layout):
    h = h_ref[...]
    q = jnp.dot(_rms(h, gpre_ref[...]).astype(BF16), wq_ref[...], preferred_element_type=F32)
    q = (q * (MEM_DIM ** -0.5)).astype(BF16)
    outs = []
    for hh in range(MEM_HEADS):
        cols = slice(hh * MEM_DIM, (hh + 1) * MEM_DIM)
        if per_head_layout:
            mk, mv = mk_ref[0, :, hh, :], mv_ref[0, :, hh, :]
        else:
            mk, mv = mk_ref[0, :, cols], mv_ref[0, :, cols]
        s = lax.dot_general(q[:, cols], mk.astype(BF16), (((1,), (1,)), ((), ())),
                            preferred_element_type=F32)
        p = jnp.exp(s - jnp.max(s, axis=-1, keepdims=True))
        p = p / jnp.sum(p, axis=-1, keepdims=True)
        outs.append(jnp.dot(p.astype(BF16), mv.astype(BF16), preferred_element_type=F32))
    o = jnp.concatenate(outs, axis=-1).astype(BF16)
    mo = jnp.dot(o, wo_ref[...], preferred_element_type=F32)
    o_ref[...] = h + _rms(mo, gpost_ref[...])


def _mem_attn(h, mk, mv, g_pre, wq, wo, g_post, rows_per_batch, tm):
    n = h.shape[0]
    tiles_per_batch = rows_per_batch // tm
    per_head_layout = mk.ndim == 4
    row = pl.BlockSpec((tm, D_MODEL), lambda i: (i, 0))
    if per_head_layout:
        mem = pl.BlockSpec((1,) + mk.shape[1:], lambda i: (i // tiles_per_batch, 0, 0, 0))
    else:
        mem = pl.BlockSpec((1,) + mk.shape[1:], lambda i: (i // tiles_per_batch, 0, 0))
    wspec = pl.BlockSpec((D_MODEL, D_MODEL), lambda i: (0, 0))
    gspec = pl.BlockSpec((1, D_MODEL), lambda i: (0, 0))
    return pl.pallas_call(
        functools.partial(_mem_attn_kernel, per_head_layout=per_head_layout),
        grid=(n // tm,),
        in_specs=[row, mem, mem, gspec, wspec, wspec, gspec],
        out_specs=row,
        out_shape=jax.ShapeDtypeStruct((n, D_MODEL), F32),
        compiler_params=_compiler_params(("arbitrary",)),
        name="mem_attn",
    )(h, mk, mv, g_pre, wq, wo, g_post)


def _mlp_kernel(h_ref, gpre_ref, wup_ref, wdown_ref, gpost_ref, o_ref, xn_scr, acc_scr):
    j = pl.program_id(1)

    @pl.when(j == 0)
    def _():
        xn_scr[...] = _rms(h_ref[...], gpre_ref[...]).astype(BF16)
        acc_scr[...] = jnp.zeros_like(acc_scr)

    u = jnp.dot(xn_scr[...], wup_ref[...], preferred_element_type=F32)
    u = jnp.square(jnp.maximum(u, 0.0)).astype(BF16)
    acc_scr[...] += jnp.dot(u, wdown_ref[...], preferred_element_type=F32)

    @pl.when(j == pl.num_programs(1) - 1)
    def _():
        o_ref[...] = h_ref[...] + _rms(acc_scr[...], gpost_ref[...])


def _mlp(h, g_pre, wup, wdown, g_post, tm):
    n = h.shape[0]
    tf = D_MODEL
    row = pl.BlockSpec((tm, D_MODEL), lambda i, j: (i, 0))
    gspec = pl.BlockSpec((1, D_MODEL), lambda i, j: (0, 0))
    return pl.pallas_call(
        _mlp_kernel,
        grid=(n // tm, D_FF // tf),
        in_specs=[row, gspec,
                  pl.BlockSpec((D_MODEL, tf), lambda i, j: (0, j)),
                  pl.BlockSpec((tf, D_MODEL), lambda i, j: (j, 0)),
                  gspec],
        out_specs=row,
        out_shape=jax.ShapeDtypeStruct((n, D_MODEL), F32),
        scratch_shapes=[pltpu.VMEM((tm, D_MODEL), BF16), pltpu.VMEM((tm, D_MODEL), F32)],
        compiler_params=_compiler_params(("arbitrary", "arbitrary")),
        name="mlp",
    )(h, g_pre, wup, wdown, g_post)


def _post_attn(o_a, o_b, proj, x, seq, mem_kv, p):
    tm = math.gcd(x.shape[0], 512)
    h = _merge(o_a, o_b, proj, x, p["w_branch_a"], p["w_branch_b"], p["w_out"], p["w_post_mix"], tm)
    mk, mv = mem_kv
    h = _mem_attn(h, mk, mv, p["w_pre_mem"], p["w_mq"], p["w_mo"], p["w_post_mem"], seq,
                  math.gcd(seq, 512))
    return _mlp(h, p["w_pre_mlp"], p["w_up"], p["w_down"], p["w_post_mlp"], tm)


def kernel(x_prompt, x_sample, cache_attn_k, cache_attn_v, state_hgrn, cache_mem_k, cache_mem_v,
           page_table, mem_prompt, w_pre_mix, w_in, hgrn_lb_logits, w_hgrn_gnorm,
           lambda_q1, lambda_k1, lambda_q2, lambda_k2, w_subln, w_branch_a, w_branch_b, w_out,
           w_post_mix, w_pre_mem, w_mem_norm, w_mq, w_mk, w_mv, w_mo, w_post_mem,
           w_pre_mlp, w_up, w_down, w_post_mlp):
    depth = w_in.shape[0]
    assert depth == 1, "single-layer step"
    l = 0
    batch, seq, _ = x_prompt.shape
    n_seq, n_new, _ = x_sample.shape
    mem_len = mem_prompt.shape[1]
    lam_init = 0.8 - 0.6 * math.exp(-0.3 * l)

    def gain(w):
        return w[l].reshape(1, -1)

    def weight(w):
        return w[l].astype(BF16)

    p = {
        "w_pre_mix": gain(w_pre_mix),
        "w_in": jnp.transpose(weight(w_in).reshape(D_MODEL, N_GROUPS, D_MODEL), (1, 0, 2)),
        "lb_logits": hgrn_lb_logits,
        "w_hgrn_gnorm": gain(w_hgrn_gnorm), "w_subln": gain(w_subln),
        "lam_params": jnp.stack([lambda_q1[l], lambda_k1[l], lambda_q2[l], lambda_k2[l]]),
        "w_branch_a": weight(w_branch_a), "w_branch_b": weight(w_branch_b), "w_out": weight(w_out),
        "w_post_mix": gain(w_post_mix), "w_pre_mem": gain(w_pre_mem), "w_mq": weight(w_mq),
        "w_mo": weight(w_mo), "w_post_mem": gain(w_post_mem), "w_pre_mlp": gain(w_pre_mlp),
        "w_up": weight(w_up), "w_down": weight(w_down), "w_post_mlp": gain(w_post_mlp),
    }

    memx = mem_prompt.reshape(batch * mem_len, D_MODEL)
    tm_mem = math.gcd(batch * mem_len, 512)
    mk_p = _norm_matmul(memx, gain(w_mem_norm), weight(w_mk), tm_mem).reshape(batch, mem_len, D_MODEL)
    mv_p = _norm_matmul(memx, gain(w_mem_norm), weight(w_mv), tm_mem).reshape(batch, mem_len, D_MODEL)
    xp = x_prompt.reshape(batch * seq, D_MODEL)
    proj_p, kt_p = _in_proj(xp, p["w_pre_mix"], p["w_in"], batch, seq, transpose_k=True)
    oa_p, s_p = _hgrn(proj_p, p["lb_logits"], p["w_hgrn_gnorm"], None, batch, seq)

    xs = x_sample.reshape(n_seq * n_new, D_MODEL)
    proj_s, _ = _in_proj(xs, p["w_pre_mix"], p["w_in"], n_seq, n_new, transpose_k=False)
    oa_s, s_s = _hgrn(proj_s, p["lb_logits"], p["w_hgrn_gnorm"], state_hgrn[l], n_seq, n_new)

    ob_p, ob_s = _fused_attn(proj_p, kt_p, proj_s, cache_attn_k[l], cache_attn_v[l], page_table,
                             p["lam_params"], p["w_subln"], batch, seq, n_seq, n_new, lam_init)
    hp = _post_attn(oa_p, ob_p, proj_p, xp, seq, (mk_p, mv_p), p)
    hs = _post_attn(oa_s, ob_s, proj_s, xs, n_new, (cache_mem_k[l], cache_mem_v[l]), p)
    k_p = jnp.transpose(kt_p.reshape(batch, N_HEADS, 2, QK_DIM, seq), (0, 4, 1, 2, 3))
    v_p = proj_p[COL_VB].reshape(batch, seq, N_HEADS, HEAD_DIM)
    k_s = proj_s[COL_KB].reshape(n_seq, n_new, N_HEADS, 2, QK_DIM)
    v_s = proj_s[COL_VB].reshape(n_seq, n_new, N_HEADS, HEAD_DIM)

    return (hp.reshape(batch, seq, D_MODEL), hs.reshape(n_seq, n_new, D_MODEL),
            k_p[None], v_p[None], s_p[None],
            mk_p.reshape(1, batch, mem_len, MEM_HEADS, MEM_DIM),
            mv_p.reshape(1, batch, mem_len, MEM_HEADS, MEM_DIM),
            k_s[None], v_s[None], s_s[None])
```
